```python
import functools
import jax
import jax.numpy as jnp
from jax import lax
import numpy as np

D_MODEL = 1024
BATCH = 8
SEQ = 4096
DEPTH = 1
DEC_BATCH = 128
DEC_SEQ = 1
PAST_LEN = 8192
PAGE_SIZE = 128

MIX_WIDTH = D_MODEL
POOL_WIDTH = MIX_WIDTH // 2
POOL_WINDOWS = (2, 4, 8, 16)
POOL_GROUPS = len(POOL_WINDOWS)
POOL_GROUP_WIDTH = POOL_WIDTH // POOL_GROUPS
POOL_BUF = max(POOL_WINDOWS) - 1
ATTN_WIDTH = MIX_WIDTH - POOL_WIDTH
HEAD_DIM = 64
N_HEADS = ATTN_WIDTH // HEAD_DIM
ROPE_DIM = HEAD_DIM // 4
ROPE_THETA = 500000.0
MOBA_BLOCK = 256
MOBA_TOPK = 3
Q_CHUNK = 128
N_MEM = 256
MEM_HEADS = 4
MEM_HEAD_DIM = 128
MEM_WIDTH = MEM_HEADS * MEM_HEAD_DIM
N_GROUPS = 4
EXPERTS_PER_GROUP = 8
N_EXPERTS = N_GROUPS * EXPERTS_PER_GROUP
EXPERT_TOPK = 2
EXPERT_HIDDEN = 256
RMS_EPS = 1e-6
NEG_INF = -1e30

kernel_name = 'hybrid_pool_moba_hmoe_step'


def rms_norm(x, g):
    xf = x.astype(jnp.float32)
    y = xf * lax.rsqrt(jnp.mean(xf * xf, axis=-1, keepdims=True) + RMS_EPS)
    return (y * g.astype(jnp.float32)).astype(x.dtype)


def partial_rope(x, pos):
    half = ROPE_DIM // 2
    inv_freq = ROPE_THETA ** (-jnp.arange(half, dtype=jnp.float32) / half)
    ang = pos.astype(jnp.float32)[:, None] * inv_freq[None, :]
    cos = jnp.cos(ang)[None, :, None, :]
    sin = jnp.sin(ang)[None, :, None, :]
    xr = x[..., :ROPE_DIM].astype(jnp.float32)
    x1, x2 = xr[..., :half], xr[..., half:]
    rot = jnp.concatenate([x1 * cos - x2 * sin, x2 * cos + x1 * sin], axis=-1)
    return jnp.concatenate([rot.astype(x.dtype), x[..., ROPE_DIM:]], axis=-1)


def split_mix_projection(h, w_in):
    b, n = h.shape[:2]
    proj = jnp.einsum('bnd,de->bne', h, w_in)
    u = proj[..., :POOL_WIDTH]
    o = POOL_WIDTH
    q = proj[..., o:o + ATTN_WIDTH].reshape(b, n, N_HEADS, HEAD_DIM)
    k = proj[..., o + ATTN_WIDTH:o + 2 * ATTN_WIDTH].reshape(b, n, N_HEADS, HEAD_DIM)
    v = proj[..., o + 2 * ATTN_WIDTH:o + 3 * ATTN_WIDTH].reshape(b, n, N_HEADS, HEAD_DIM)
    return u, q, k, v


def multiscale_pool(u_ext, pos_new, w_pool, s_pool):
    n = u_ext.shape[1] - POOL_BUF
    uf = u_ext.astype(jnp.float32)
    cs = jnp.cumsum(jnp.pad(uf, ((0, 0), (1, 0), (0, 0))), axis=1)
    u_new = uf[:, POOL_BUF:]
    outs = []
    for g, w in enumerate(POOL_WINDOWS):
        lo, hi = g * POOL_GROUP_WIDTH, (g + 1) * POOL_GROUP_WIDTH
        wsum = cs[:, POOL_BUF + 1:, lo:hi] - cs[:, POOL_BUF + 1 - w:POOL_BUF + 1 - w + n, lo:hi]
        cnt = jnp.minimum(w, pos_new + 1).astype(jnp.float32)[None, :, None]
        d = wsum / cnt - u_new[..., lo:hi]
        outs.append(jnp.einsum('bnc,ce->bne', d, w_pool[g].astype(jnp.float32)))
    z = jnp.concatenate(outs, axis=-1) * s_pool.astype(jnp.float32)
    return z.astype(u_ext.dtype)


def two_part_softmax(s_a, m_a, s_b, m_b):
    na = s_a.shape[-1]
    s = jnp.concatenate([jnp.where(m_a, s_a, NEG_INF), jnp.where(m_b, s_b, NEG_INF)], axis=-1)
    p = jax.nn.softmax(s, axis=-1)
    return p[..., :na], p[..., na:]


def moba_prompt(q, k, v):
    b, s = q.shape[:2]
    nb = -(-s // MOBA_BLOCK)
    ksel = min(MOBA_TOPK, nb)
    pad = nb * MOBA_BLOCK - s

    def to_blocks(t):
        t = jnp.pad(t, ((0, 0), (0, pad), (0, 0), (0, 0)))
        return t.reshape(b, nb, MOBA_BLOCK, N_HEADS, HEAD_DIM).transpose(0, 3, 1, 2, 4)

    kb, vb = to_blocks(k), to_blocks(v)
    k_mean = jnp.mean(kb.astype(jnp.float32), axis=3)
    qh = q.transpose(0, 2, 1, 3)
    blocks = jnp.arange(nb)
    offs = jnp.arange(MOBA_BLOCK)
    bi = jnp.arange(b)[:, None, None, None]
    hi = jnp.arange(N_HEADS)[None, :, None, None]
    scale = HEAD_DIM ** -0.5

    def attend_chunk(c):
        q0 = c * Q_CHUNK
        qc = lax.dynamic_slice_in_dim(qh, q0, Q_CHUNK, axis=2)
        qpos = q0 + jnp.arange(Q_CHUNK)
        own = q0 // MOBA_BLOCK
        gate = jnp.einsum('bhqd,bhjd->bhqj', qc.astype(jnp.float32), k_mean)
        gate = jnp.where(blocks < own, gate, -jnp.inf)
        _, top_idx = lax.top_k(gate, ksel)
        k_sel = kb[bi, hi, top_idx].reshape(b, N_HEADS, Q_CHUNK, ksel * MOBA_BLOCK, HEAD_DIM)
        v_sel = vb[bi, hi, top_idx].reshape(b, N_HEADS, Q_CHUNK, ksel * MOBA_BLOCK, HEAD_DIM)
        k_own = lax.dynamic_index_in_dim(kb, own, axis=2, keepdims=False)
        v_own = lax.dynamic_index_in_dim(vb, own, axis=2, keepdims=False)
        s_sel = jnp.einsum('bhqd,bhqkd->bhqk', qc, k_sel).astype(jnp.float32) * scale
        s_own = jnp.einsum('bhqd,bhkd->bhqk', qc, k_own).astype(jnp.float32) * scale
        m_sel = jnp.repeat(jnp.arange(ksel) < own, MOBA_BLOCK)
        m_own = (own * MOBA_BLOCK + offs)[None, :] <= qpos[:, None]
        p_sel, p_own = two_part_softmax(s_sel, m_sel, s_own, m_own)
        return (jnp.einsum('bhqk,bhqkd->bhqd', p_sel.astype(v_sel.dtype), v_sel)
                + jnp.einsum('bhqk,bhkd->bhqd', p_own.astype(v_own.dtype), v_own))

    out = lax.map(attend_chunk, jnp.arange(s // Q_CHUNK, dtype=jnp.int32))
    return out.transpose(1, 0, 3, 2, 4).reshape(b, s, N_HEADS, HEAD_DIM)


def moba_sample(layer, cache_k, cache_v, page_table, q, k_new, v_new):
    bd, n = q.shape[:2]
    n_pages = page_table.shape[1]
    past_len = n_pages * PAGE_SIZE
    total = past_len + n
    nb = -(-total // MOBA_BLOCK)
    ksel = min(MOBA_TOPK, nb)
    qpos = past_len + jnp.arange(n, dtype=jnp.int32)
    own = qpos // MOBA_BLOCK
    blocks = jnp.arange(nb)
    page_sum = jnp.sum(cache_k[layer, page_table].astype(jnp.float32), axis=2)
    page_blk = (jnp.arange(n_pages) * PAGE_SIZE) // MOBA_BLOCK
    onehot_pages = (page_blk[:, None] == blocks[None, :]).astype(jnp.float32)
    onehot_new = (own[:, None] == blocks[None, :]).astype(jnp.float32)
    k_sum = (jnp.einsum('bphd,pj->bhjd', page_sum, onehot_pages)
             + jnp.einsum('bnhd,nj->bhjd', k_new.astype(jnp.float32), onehot_new))
    k_mean = k_sum / MOBA_BLOCK
    qh = q.transpose(0, 2, 1, 3)
    gate = jnp.einsum('bhqd,bhjd->bhqj', qh.astype(jnp.float32), k_mean)
    gate = jnp.where(blocks[None, :] < own[:, None], gate, -jnp.inf)
    _, top_idx = lax.top_k(gate, ksel)
    offs = jnp.arange(MOBA_BLOCK)
    sel_pos = (top_idx[..., None] * MOBA_BLOCK + offs).reshape(bd, N_HEADS, n, ksel * MOBA_BLOCK)
    own_rows = own[:, None] * MOBA_BLOCK + offs
    own_pos = jnp.broadcast_to(own_rows, (bd, N_HEADS, n, MOBA_BLOCK))
    bi = jnp.arange(bd)[:, None, None, None]
    hi = jnp.arange(N_HEADS)[None, :, None, None]

    def fetch(cache, new, pos):
        pc = jnp.clip(pos, 0, past_len - 1)
        phys = page_table[bi, pc // PAGE_SIZE]
        rows_cache = cache[layer, phys, pc % PAGE_SIZE, hi]
        pn = jnp.clip(pos - past_len, 0, n - 1)
        rows_new = new[bi, pn, hi]
        return jnp.where((pos < past_len)[..., None], rows_cache, rows_new.astype(rows_cache.dtype))

    k_sel, v_sel = fetch(cache_k, k_new, sel_pos), fetch(cache_v, v_new, sel_pos)
    k_own, v_own = fetch(cache_k, k_new, own_pos), fetch(cache_v, v_new, own_pos)
    scale = HEAD_DIM ** -0.5
    s_sel = jnp.einsum('bhqd,bhqkd->bhqk', qh, k_sel).astype(jnp.float32) * scale
    s_own = jnp.einsum('bhqd,bhqkd->bhqk', qh, k_own).astype(jnp.float32) * scale
    m_sel = jnp.repeat(jnp.arange(ksel)[None, :] < own[:, None], MOBA_BLOCK, axis=1)
    m_own = own_rows <= qpos[:, None]
    p_sel, p_own = two_part_softmax(s_sel, m_sel, s_own, m_own)
    o = (jnp.einsum('bhqk,bhqkd->bhqd', p_sel.astype(v_sel.dtype), v_sel)
         + jnp.einsum('bhqk,bhqkd->bhqd', p_own.astype(v_own.dtype), v_own))
    return o.transpose(0, 2, 1, 3)


def memory_kv(mem, g_mem_in, w_mk, w_mv):
    b, m = mem.shape[:2]
    hm = rms_norm(mem, g_mem_in)
    mk = jnp.einsum('bmd,de->bme', hm, w_mk).reshape(b, m, MEM_HEADS, MEM_HEAD_DIM)
    mv = jnp.einsum('bmd,de->bme', hm, w_mv).reshape(b, m, MEM_HEADS, MEM_HEAD_DIM)
    return mk, mv


def memory_cross_attention(h, mk, mv, w_mq, w_mo):
    b, n = h.shape[:2]
    q = jnp.einsum('bnd,de->bne', h, w_mq).reshape(b, n, MEM_HEADS, MEM_HEAD_DIM)
    s = jnp.einsum('bnhd,bmhd->bhnm', q, mk).astype(jnp.float32) * MEM_HEAD_DIM ** -0.5
    p = jax.nn.softmax(s, axis=-1)
    o = jnp.einsum('bhnm,bmhd->bnhd', p.astype(mv.dtype), mv).reshape(b, n, MEM_WIDTH)
    return jnp.einsum('bne,ed->bnd', o, w_mo)


def hierarchical_moe(h, w_rg, b_rg, w_re, b_re, w1, w3, w2):
    shp = h.shape
    t = h.reshape(-1, shp[-1])
    n = t.shape[0]
    lg = jnp.einsum('td,dg->tg', t, w_rg).astype(jnp.float32) + b_rg.astype(jnp.float32)
    grp = jnp.argmax(lg, axis=-1)
    p_grp = jnp.take_along_axis(jax.nn.softmax(lg, axis=-1), grp[:, None], axis=-1)
    le = (jnp.einsum('td,de->te', t, w_re).astype(jnp.float32) + b_re.astype(jnp.float32))
    le = le.reshape(n, N_GROUPS, EXPERTS_PER_GROUP)
    le_g = jnp.take_along_axis(le, grp[:, None, None], axis=1)[:, 0]
    top_l, top_j = lax.top_k(le_g, EXPERT_TOPK)
    gates = p_grp * jax.nn.softmax(top_l, axis=-1)
    ids = grp[:, None] * EXPERTS_PER_GROUP + top_j
    combine = jnp.sum((ids[..., None] == jnp.arange(N_EXPERTS)) * gates[..., None], axis=1)
    y = jnp.zeros(t.shape, jnp.float32)
    for e in range(N_EXPERTS):
        a = jax.nn.silu(t @ w1[e]) * (t @ w3[e])
        y = y + combine[:, e:e + 1] * (a @ w2[e]).astype(jnp.float32)
    return y.astype(h.dtype).reshape(shp)


def hybrid_layer(x, pos, pool_prefix, attend, mem_k, mem_v, g_mix, w_in, w_pool, s_pool, w_out,
                 g_mem, w_mq, w_mo, g_ffn, w_rg, b_rg, w_re, b_re, w1, w3, w2):
    b, n = x.shape[:2]
    h = rms_norm(x, g_mix)
    u, q, k, v = split_mix_projection(h, w_in)
    z_pool = multiscale_pool(jnp.concatenate([pool_prefix.astype(u.dtype), u], axis=1), pos, w_pool, s_pool)
    q = partial_rope(q, pos)
    k = partial_rope(k, pos)
    o_attn = attend(q, k, v).reshape(b, n, ATTN_WIDTH)
    mixed = jnp.concatenate([z_pool, o_attn.astype(z_pool.dtype)], axis=-1)
    x = x + jnp.einsum('bne,ed->bnd', mixed, w_out)
    x = x + memory_cross_attention(rms_norm(x, g_mem), mem_k, mem_v, w_mq, w_mo)
    x = x + hierarchical_moe(rms_norm(x, g_ffn), w_rg, b_rg, w_re, b_re, w1, w3, w2)
    return x, u, k, v


def setup_inputs(seed: int = 0) -> dict:
    key = jax.random.key(seed)
    ks = jax.random.split(key, 32)
    f32 = jnp.float32
    n_pages = PAST_LEN // PAGE_SIZE
    n_used = DEC_BATCH * n_pages
    n_pool = n_used + n_used // 4

    def nrm(k, shape, scale=1.0):
        return jax.random.normal(k, shape, f32) * scale

    def gain(k, shape):
        return 1.0 + 0.02 * nrm(k, shape)

    page_table = jax.random.permutation(ks[0], n_pool)[:n_used].reshape(DEC_BATCH, n_pages).astype(jnp.int32)
    return {
        'x_prompt': nrm(ks[1], (BATCH, SEQ, D_MODEL)),
        'x_sample': nrm(ks[2], (DEC_BATCH, DEC_SEQ, D_MODEL)),
        'mem_prompt': nrm(ks[3], (BATCH, N_MEM, D_MODEL)),
        'cache_k': nrm(ks[4], (DEPTH, n_pool, PAGE_SIZE, N_HEADS, HEAD_DIM)),
        'cache_v': nrm(ks[5], (DEPTH, n_pool, PAGE_SIZE, N_HEADS, HEAD_DIM)),
        'cache_mem_k': nrm(ks[6], (DEPTH, DEC_BATCH, N_MEM, MEM_HEADS, MEM_HEAD_DIM)),
        'cache_mem_v': nrm(ks[7], (DEPTH, DEC_BATCH, N_MEM, MEM_HEADS, MEM_HEAD_DIM)),
        'state_pool': nrm(ks[8], (DEPTH, DEC_BATCH, POOL_BUF, POOL_WIDTH)),
        'page_table': page_table,
        'g_mix': gain(ks[9], (DEPTH, D_MODEL)),
        'w_in': nrm(ks[10], (DEPTH, D_MODEL, POOL_WIDTH + 3 * ATTN_WIDTH), D_MODEL ** -0.5),
        'w_pool': nrm(ks[11], (DEPTH, POOL_GROUPS, POOL_GROUP_WIDTH, POOL_GROUP_WIDTH), POOL_GROUP_WIDTH ** -0.5),
        's_pool': 1.0 + 0.1 * nrm(ks[12], (DEPTH, POOL_WIDTH)),
        'w_out': nrm(ks[13], (DEPTH, MIX_WIDTH, D_MODEL), MIX_WIDTH ** -0.5),
        'g_mem': gain(ks[14], (DEPTH, D_MODEL)),
        'g_mem_in': gain(ks[15], (DEPTH, D_MODEL)),
        'w_mq': nrm(ks[16], (DEPTH, D_MODEL, MEM_WIDTH), D_MODEL ** -0.5),
        'w_mk': nrm(ks[17], (DEPTH, D_MODEL, MEM_WIDTH), D_MODEL ** -0.5),
        'w_mv': nrm(ks[18], (DEPTH, D_MODEL, MEM_WIDTH), D_MODEL ** -0.5),
        'w_mo': nrm(ks[19], (DEPTH, MEM_WIDTH, D_MODEL), MEM_WIDTH ** -0.5),
        'g_ffn': gain(ks[20], (DEPTH, D_MODEL)),
        'w_rg': nrm(ks[21], (DEPTH, D_MODEL, N_GROUPS), D_MODEL ** -0.5),
        'b_rg': nrm(ks[22], (DEPTH, N_GROUPS), 0.01),
        'w_re': nrm(ks[23], (DEPTH, D_MODEL, N_EXPERTS), D_MODEL ** -0.5),
        'b_re': nrm(ks[24], (DEPTH, N_EXPERTS), 0.01),
        'w1': nrm(ks[25], (DEPTH, N_EXPERTS, D_MODEL, EXPERT_HIDDEN), D_MODEL ** -0.5),
        'w3': nrm(ks[26], (DEPTH, N_EXPERTS, D_MODEL, EXPERT_HIDDEN), D_MODEL ** -0.5),
        'w2': nrm(ks[27], (DEPTH, N_EXPERTS, EXPERT_HIDDEN, D_MODEL), EXPERT_HIDDEN ** -0.5),
        'g_final': gain(ks[28], (D_MODEL,)),
    }


def reference(x_prompt, x_sample, mem_prompt, cache_k, cache_v, cache_mem_k, cache_mem_v, state_pool,
              page_table, g_mix, w_in, w_pool, s_pool, w_out, g_mem, g_mem_in, w_mq, w_mk, w_mv, w_mo,
              g_ffn, w_rg, b_rg, w_re, b_re, w1, w3, w2, g_final):
    b, s = x_prompt.shape[:2]
    n_new = x_sample.shape[1]
    past_len = page_table.shape[1] * PAGE_SIZE
    pos_p = jnp.arange(s, dtype=jnp.int32)
    pos_s = past_len + jnp.arange(n_new, dtype=jnp.int32)
    xp, xs = x_prompt, x_sample
    k_p_l, v_p_l, pool_p_l, mk_p_l, mv_p_l = [], [], [], [], []
    k_s_l, v_s_l, pool_s_l = [], [], []
    for l in range(DEPTH):
        lw = (g_mix[l], w_in[l], w_pool[l], s_pool[l], w_out[l], g_mem[l], w_mq[l], w_mo[l],
              g_ffn[l], w_rg[l], b_rg[l], w_re[l], b_re[l], w1[l], w3[l], w2[l])
        mk_p, mv_p = memory_kv(mem_prompt, g_mem_in[l], w_mk[l], w_mv[l])
        prefix_p = jnp.zeros((b, POOL_BUF, POOL_WIDTH), xp.dtype)
        xp, u_p, k_p, v_p = hybrid_layer(xp, pos_p, prefix_p, moba_prompt, mk_p, mv_p, *lw)
        k_p_l.append(k_p)
        v_p_l.append(v_p)
        pool_p_l.append(u_p[:, s - POOL_BUF:])
        mk_p_l.append(mk_p)
        mv_p_l.append(mv_p)
        attend_s = functools.partial(moba_sample, l, cache_k, cache_v, page_table)
        xs, u_s, k_s, v_s = hybrid_layer(xs, pos_s, state_pool[l], attend_s, cache_mem_k[l], cache_mem_v[l], *lw)
        k_s_l.append(k_s)
        v_s_l.append(v_s)
        pool_s_l.append(jnp.concatenate([state_pool[l].astype(u_s.dtype), u_s], axis=1)[:, -POOL_BUF:])
    y_prompt = rms_norm(xp, g_final)
    y_sample = rms_norm(xs, g_final)
    new_k_prompt = jnp.stack(k_p_l, axis=0)
    new_v_prompt = jnp.stack(v_p_l, axis=0)
    new_pool_prompt = jnp.stack(pool_p_l, axis=0)
    new_mem_k_prompt = jnp.stack(mk_p_l, axis=0)
    new_mem_v_prompt = jnp.stack(mv_p_l, axis=0)
    new_k_sample = jnp.stack(k_s_l, axis=0)
    new_v_sample = jnp.stack(v_s_l, axis=0)
    new_pool_sample = jnp.stack(pool_s_l, axis=0)
    return (y_prompt, y_sample, new_k_prompt, new_v_prompt, new_pool_prompt, new_mem_k_prompt,
            new_mem_v_prompt, new_k_sample, new_v_sample, new_pool_sample)
```

```python
import functools

import jax
import jax.numpy as jnp
from jax import lax
from jax.experimental import pallas as pl
from jax.experimental.pallas import tpu as pltpu

F32 = jnp.float32
BF16 = jnp.bfloat16

POOL_WINDOWS = (2, 4, 8, 16)
POOL_BUF = max(POOL_WINDOWS) - 1
HEAD_DIM = 64
ROPE_DIM = HEAD_DIM // 4
ROPE_THETA = 500000.0
MOBA_BLOCK = 256
MOBA_TOPK = 3
PAGE_SIZE = 128
MEM_HEAD_DIM = 128
N_GROUPS = 4
EXPERTS_PER_GROUP = 8
RMS_EPS = 1e-6
NEG_INF = -1e30

LANES = 128
VMEM_LIMIT = 56 * 1024 * 1024


def _params(*sem):
    return pltpu.CompilerParams(dimension_semantics=sem, vmem_limit_bytes=VMEM_LIMIT)


def _rms(x, g):
    ms = jnp.mean(x * x, axis=-1, keepdims=True)
    return x * lax.rsqrt(ms + RMS_EPS) * g


def _dot(a, b):
    return jnp.dot(a, b, preferred_element_type=F32)


def _dot_nt(a, b):
    return lax.dot_general(a, b, (((1,), (1,)), ((), ())), preferred_element_type=F32)


def _dot_exact(a, b):
    return jnp.dot(a, b, preferred_element_type=F32, precision=lax.Precision.HIGHEST)


def _idiv(x, n):
    shift = n.bit_length() - 1
    assert 1 << shift == n
    return jnp.right_shift(x, shift)


def _const_spec(shape):
    return pl.BlockSpec(shape, lambda *_: (0,) * len(shape))


def _diag_mask(n):
    return (lax.broadcasted_iota(jnp.int32, (n, n), 0) == lax.broadcasted_iota(jnp.int32, (n, n), 1))


def _inproj_kernel(x_ref, g_ref, wuq_ref, wkv_ref, cos_ref, sa_ref, sb_ref, cos_t_ref, sa_t_ref,
                   sb_t_ref, u_ref, q_ref, kt_ref, vt_ref, *blk_refs):
    width = u_ref.shape[-1]
    half = ROPE_DIM // 2
    h = _rms(x_ref[...], g_ref[...]).astype(BF16)
    uq = _dot(h, wuq_ref[...])
    kvt = _dot_nt(wkv_ref[...], h)

    q = uq[:, width:]
    q_out = []
    for c in range(width // LANES):
        tc = q[:, c * LANES:(c + 1) * LANES]
        q_out.append(tc * cos_ref[...] + pltpu.roll(tc, LANES - half, axis=1) * sa_ref[...]
                     + pltpu.roll(tc, half, axis=1) * sb_ref[...])
    k_out = []
    for c in range(width // LANES):
        tc = kvt[c * LANES:(c + 1) * LANES, :]
        k_out.append(tc * cos_t_ref[...] + pltpu.roll(tc, LANES - half, axis=0) * sa_t_ref[...]
                     + pltpu.roll(tc, half, axis=0) * sb_t_ref[...])
    kt = jnp.concatenate(k_out, axis=0)
    vt = kvt[width:, :]

    u_ref[...] = uq[:, :width]
    q_ref[...] = jnp.concatenate(q_out, axis=1) * (HEAD_DIM ** -0.5)
    kt_ref[...] = kt
    vt_ref[...] = vt
    if blk_refs:
        ktb_ref, vtb_ref = blk_refs
        for jj in range(ktb_ref.shape[0]):
            sl = slice(jj * MOBA_BLOCK, (jj + 1) * MOBA_BLOCK)
            ktb_ref[jj] = kt[:, sl].astype(BF16)
            vtb_ref[jj] = vt[:, sl].astype(BF16)


def _rope_tables(pos):
    half = ROPE_DIM // 2
    inv_freq = ROPE_THETA ** (-jnp.arange(half, dtype=F32) / half)
    ang = pos.astype(F32)[:, None] * inv_freq[None, :]
    cos, sin = jnp.cos(ang), jnp.sin(ang)
    n = pos.shape[0]
    pad = jnp.zeros((n, HEAD_DIM - ROPE_DIM), F32)
    cos_h = jnp.concatenate([cos, cos, pad + 1.0], axis=1)
    sa_h = jnp.concatenate([-sin, jnp.zeros_like(sin), pad], axis=1)
    sb_h = jnp.concatenate([jnp.zeros_like(sin), sin, pad], axis=1)
    rep = LANES // HEAD_DIM
    return tuple(jnp.tile(t, (1, rep)) for t in (cos_h, sa_h, sb_h))


def _inproj(x, pos, g_mix, wuq_bf, wkv_t_bf, tm, attn_copies):
    b, s, d = x.shape
    width = wuq_bf.shape[1] // 2
    tabs = _rope_tables(pos)
    tabs_t = tuple(t.T for t in tabs)
    tok = lambda last: pl.BlockSpec((None, tm, last), lambda bi, i: (bi, i, 0))
    feat = pl.BlockSpec((None, width, tm), lambda bi, i: (bi, 0, i))
    tab = pl.BlockSpec((tm, LANES), lambda bi, i: (i, 0))
    tab_t = pl.BlockSpec((LANES, tm), lambda bi, i: (0, i))
    out_specs = [tok(width), tok(width), feat, feat]
    out_shape = [jax.ShapeDtypeStruct((b, s, width), F32)] * 2 + [jax.ShapeDtypeStruct((b, width, s), F32)] * 2
    if attn_copies:
        per = tm // MOBA_BLOCK
        blk = pl.BlockSpec((None, per, width, MOBA_BLOCK), lambda bi, i: (bi, i, 0, 0))
        out_specs += [blk, blk]
        out_shape += [jax.ShapeDtypeStruct((b, s // MOBA_BLOCK, width, MOBA_BLOCK), BF16)] * 2
    return pl.pallas_call(
        _inproj_kernel,
        grid=(b, s // tm),
        in_specs=[tok(d), _const_spec((1, d)), _const_spec(wuq_bf.shape),
                  _const_spec(wkv_t_bf.shape), tab, tab, tab, tab_t, tab_t, tab_t],
        out_specs=out_specs,
        out_shape=out_shape,
        compiler_params=_params("parallel", "parallel"),
        name="inproj",
    )(x, g_mix.reshape(1, d), wuq_bf, wkv_t_bf, *tabs, *tabs_t)


def _moba_prompt_kernel(q_ref, kt_ref, vt_ref, o_ref, kmean_sc, m_sc, l_sc, acc_sc):
    i = pl.program_id(2)
    nb = kt_ref.shape[0]
    blk = q_ref.shape[0]
    col = lax.broadcasted_iota(jnp.int32, (blk, nb), 1)

    @pl.when(i == 0)
    def _():
        lane_nb = lax.broadcasted_iota(jnp.int32, (LANES, nb), 1)
        km = jnp.zeros((LANES, nb), F32)
        for j in range(nb):
            mean_j = jnp.sum(kt_ref[j].astype(F32), axis=1, keepdims=True) / blk
            km = jnp.where(lane_nb == j, mean_j, km)
        kmean_sc[...] = km

    q = q_ref[...]
    lane = lax.broadcasted_iota(jnp.int32, (1, LANES), 1)
    row_q = lax.broadcasted_iota(jnp.int32, (blk, blk), 0)
    col_k = lax.broadcasted_iota(jnp.int32, (blk, blk), 1)
    k_own = kt_ref[i]
    v_own = vt_ref[i]

    for hd in range(2):
        qm = jnp.where(_idiv(lane, HEAD_DIM) == hd, q, 0.0)
        qb = qm.astype(BF16)
        gate = jnp.where(col < i, _dot_exact(qm, kmean_sc[...]), -jnp.inf)
        sel = jnp.zeros((blk, nb), F32)
        for _ in range(MOBA_TOPK):
            gmax = jnp.max(gate, axis=1, keepdims=True)
            first = jnp.min(jnp.where(gate == gmax, col, nb), axis=1, keepdims=True)
            pick = col == first
            sel = jnp.where(pick, 1.0, sel)
            gate = jnp.where(pick, -jnp.inf, gate)
        sel = jnp.where(col < i, sel, 0.0)

        s = jnp.where(col_k <= row_q, _dot(qb, k_own), NEG_INF)
        m0 = jnp.max(s, axis=1, keepdims=True)
        p = jnp.exp(s - m0)
        m_sc[hd] = jnp.broadcast_to(m0, (blk, LANES))
        l_sc[hd] = jnp.broadcast_to(jnp.sum(p, axis=1, keepdims=True), (blk, LANES))
        acc_sc[hd] = _dot_nt(p.astype(BF16), v_own)

        def body(j, carry):
            chosen = jnp.sum(jnp.where(col == j, sel, 0.0), axis=1, keepdims=True) > 0.5
            sj = jnp.where(chosen, _dot(qb, kt_ref[j]), NEG_INF)
            m_old = m_sc[hd][:, :1]
            m_new = jnp.maximum(m_old, jnp.max(sj, axis=1, keepdims=True))
            alpha = jnp.exp(m_old - m_new)
            pj = jnp.exp(sj - m_new)
            l_new = alpha * l_sc[hd][:, :1] + jnp.sum(pj, axis=1, keepdims=True)
            acc_sc[hd] = alpha * acc_sc[hd] + _dot_nt(pj.astype(BF16), vt_ref[j])
            m_sc[hd] = jnp.broadcast_to(m_new, (blk, LANES))
            l_sc[hd] = jnp.broadcast_to(l_new, (blk, LANES))
            return carry

        lax.fori_loop(0, i, body, 0)

    out = jnp.where(lane < HEAD_DIM, acc_sc[0] / l_sc[0], acc_sc[1] / l_sc[1])
    o_ref[...] = out.astype(o_ref.dtype)


def _moba_prompt(q, ktb, vtb):
    b, s, width = q.shape
    nb = s // MOBA_BLOCK
    blk = MOBA_BLOCK
    qspec = pl.BlockSpec((None, blk, LANES), lambda bi, hp, i: (bi, i, hp))
    kvspec = pl.BlockSpec((None, nb, LANES, blk), lambda bi, hp, i: (bi, 0, hp, 0))
    return pl.pallas_call(
        _moba_prompt_kernel,
        grid=(b, width // LANES, nb),
        in_specs=[qspec, kvspec, kvspec],
        out_specs=qspec,
        out_shape=jax.ShapeDtypeStruct((b, s, width), BF16),
        scratch_shapes=[pltpu.VMEM((LANES, nb), F32), pltpu.VMEM((2, blk, LANES), F32),
                        pltpu.VMEM((2, blk, LANES), F32), pltpu.VMEM((2, blk, LANES), F32)],
        compiler_params=_params("parallel", "parallel", "arbitrary"),
        name="moba_prompt",
    )(q, ktb, vtb)


def _memkv_kernel(mem_ref, g_ref, w_ref, mk_ref, mv_ref):
    width = mk_ref.shape[-1]
    hm = _rms(mem_ref[...], g_ref[...]).astype(BF16)
    kv = _dot(hm, w_ref[...])
    mk_ref[...] = kv[:, :width]
    mv_ref[...] = kv[:, width:]


def _memkv(mem, g_mem_in, w_mkv_bf):
    b, m, d = mem.shape
    width = w_mkv_bf.shape[1] // 2
    out = jax.ShapeDtypeStruct((b, m, width), F32)
    return pl.pallas_call(
        _memkv_kernel,
        grid=(b,),
        in_specs=[pl.BlockSpec((None, m, d), lambda bi: (bi, 0, 0)), _const_spec((1, d)),
                  _const_spec(w_mkv_bf.shape)],
        out_specs=[pl.BlockSpec((None, m, width), lambda bi: (bi, 0, 0))] * 2,
        out_shape=[out, out],
        compiler_params=_params("parallel"),
        name="memkv",
    )(mem, g_mem_in.reshape(1, d), w_mkv_bf)


def _pool_mix(z_in_groups, wp_ref, sp_ref):
    zs = [_dot(d.astype(BF16), wp_ref[g]) for g, d in enumerate(z_in_groups)]
    return jnp.concatenate(zs, axis=1) * sp_ref[...]


def _cross_attend(x1, gm_ref, wq_ref, mk, mv, wo_ref):
    hm = _rms(x1, gm_ref[...]).astype(BF16)
    qm = _dot(hm, wq_ref[...])
    outs = []
    for hh in range(qm.shape[1] // MEM_HEAD_DIM):
        sl = slice(hh * MEM_HEAD_DIM, (hh + 1) * MEM_HEAD_DIM)
        s = _dot_nt(qm[:, sl].astype(BF16), mk[:, sl]) * (MEM_HEAD_DIM ** -0.5)
        p = jnp.exp(s - jnp.max(s, axis=1, keepdims=True))
        p = p / jnp.sum(p, axis=1, keepdims=True)
        outs.append(_dot(p.astype(BF16), mv[:, sl]))
    o = jnp.concatenate(outs, axis=1).astype(BF16)
    return x1 + _dot(o, wo_ref[...])


def _mix_prompt_kernel(x_ref, u_ref, halo_ref, oa_ref, wp_ref, sp_ref, wout_ref, gm_ref, wq_ref,
                       mk_ref, mv_ref, wo_ref, x2_ref):
    i = pl.program_id(1)
    tm, width = u_ref.shape
    gw = width // len(POOL_WINDOWS)
    halo_rows = halo_ref.shape[0]
    u = u_ref[...]
    halo = jnp.where(i > 0, halo_ref[...], 0.0)
    ext = jnp.concatenate([halo, u], axis=0)
    pos = i * tm + lax.broadcasted_iota(jnp.int32, (tm, 1), 0)
    groups = []
    for g, w in enumerate(POOL_WINDOWS):
        a = ext[:, g * gw:(g + 1) * gw]
        span = 1
        while span < w:
            a = a + pltpu.roll(a, span, axis=0)
            span *= 2
        cnt = jnp.minimum(w, pos + 1).astype(F32)
        groups.append(a[halo_rows:] / cnt - u[:, g * gw:(g + 1) * gw])
    z = _pool_mix(groups, wp_ref, sp_ref).astype(BF16)
    x1 = x_ref[...] + _dot(z, wout_ref[:width, :]) + _dot(oa_ref[...], wout_ref[width:, :])
    x2_ref[...] = _cross_attend(x1, gm_ref, wq_ref, mk_ref[...].astype(BF16),
                                mv_ref[...].astype(BF16), wo_ref)


def _mix_prompt(x, u, o_attn, wp_bf, s_pool, wout_bf, g_mem, wq_bf, mk, mv, wo_bf, tm):
    b, s, d = x.shape
    width = u.shape[-1]
    m = mk.shape[1]
    halo = 16
    tok = lambda last: pl.BlockSpec((None, tm, last), lambda bi, i: (bi, i, 0))
    halo_spec = pl.BlockSpec((None, halo, width),
                             lambda bi, i: (bi, jnp.maximum(i * (tm // halo) - 1, 0), 0))
    mem_spec = pl.BlockSpec((None, m, mk.shape[-1]), lambda bi, i: (bi, 0, 0))
    return pl.pallas_call(
        _mix_prompt_kernel,
        grid=(b, s // tm),
        in_specs=[tok(d), tok(width), halo_spec, tok(width), _const_spec(wp_bf.shape),
                  _const_spec((1, width)), _const_spec(wout_bf.shape), _const_spec((1, d)),
                  _const_spec(wq_bf.shape), mem_spec, mem_spec, _const_spec(wo_bf.shape)],
        out_specs=tok(d),
        out_shape=jax.ShapeDtypeStruct((b, s, d), F32),
        compiler_params=_params("parallel", "parallel"),
        name="mix_prompt",
    )(x, u, u, o_attn, wp_bf, s_pool.reshape(1, width), wout_bf, g_mem.reshape(1, d), wq_bf,
      mk, mv, wo_bf)


def _moe_kernel(x_ref, gf_ref, wr_ref, br_ref, w1_ref, w3_ref, w2_ref, gfin_ref, y_ref,
                h_sc, comb_sc, acc_sc):
    e = pl.program_id(1)
    n_exp = pl.num_programs(1)
    tm = x_ref.shape[0]
    lane = lax.broadcasted_iota(jnp.int32, (tm, LANES), 1)

    @pl.when(e == 0)
    def _():
        t = _rms(x_ref[...], gf_ref[...])
        h_sc[...] = t.astype(BF16)
        logits = _dot_exact(t, wr_ref[...]) + br_ref[...]
        n_e = N_GROUPS * EXPERTS_PER_GROUP
        is_g = (lane >= n_e) & (lane < n_e + N_GROUPS)
        lg = jnp.where(is_g, logits, -jnp.inf)
        gmax = jnp.max(lg, axis=1, keepdims=True)
        grp = jnp.min(jnp.where(lg == gmax, lane, LANES), axis=1, keepdims=True) - n_e
        p_grp = 1.0 / jnp.sum(jnp.where(is_g, jnp.exp(logits - gmax), 0.0), axis=1, keepdims=True)
        le = jnp.where(_idiv(lane, EXPERTS_PER_GROUP) == grp, logits, -jnp.inf)
        t1 = jnp.max(le, axis=1, keepdims=True)
        i1 = jnp.min(jnp.where(le == t1, lane, LANES), axis=1, keepdims=True)
        le2 = jnp.where(lane == i1, -jnp.inf, le)
        t2 = jnp.max(le2, axis=1, keepdims=True)
        i2 = jnp.min(jnp.where(le2 == t2, lane, LANES), axis=1, keepdims=True)
        r = jnp.exp(t2 - t1)
        g1 = 1.0 / (1.0 + r)
        g2 = r / (1.0 + r)
        comb_sc[...] = p_grp * (jnp.where(lane == i1, g1, 0.0) + jnp.where(lane == i2, g2, 0.0))
        acc_sc[...] = jnp.zeros_like(acc_sc)

    h = h_sc[...]
    a1 = _dot(h, w1_ref[...])
    a3 = _dot(h, w3_ref[...])
    c = jnp.sum(jnp.where(lane == e, comb_sc[...], 0.0), axis=1, keepdims=True)
    a = (a1 * (1.0 / (1.0 + jnp.exp(-a1)))) * a3
    acc_sc[...] += c * _dot(a.astype(BF16), w2_ref[...])

    @pl.when(e == n_exp - 1)
    def _():
        y_ref[...] = _rms(x_ref[...] + acc_sc[...], gfin_ref[...])


def _moe(x, g_ffn, wr, br, w1_bf, w3_bf, w2_bf, g_final, tm):
    t, d = x.shape
    n_exp, _, hid = w1_bf.shape
    tok = pl.BlockSpec((tm, d), lambda i, e: (i, 0))
    return pl.pallas_call(
        _moe_kernel,
        grid=(t // tm, n_exp),
        in_specs=[tok, _const_spec((1, d)), _const_spec(wr.shape), _const_spec((1, LANES)),
                  pl.BlockSpec((None, d, hid), lambda i, e: (e, 0, 0)),
                  pl.BlockSpec((None, d, hid), lambda i, e: (e, 0, 0)),
                  pl.BlockSpec((None, hid, d), lambda i, e: (e, 0, 0)),
                  _const_spec((1, d))],
        out_specs=tok,
        out_shape=jax.ShapeDtypeStruct((t, d), F32),
        scratch_shapes=[pltpu.VMEM((tm, d), BF16), pltpu.VMEM((tm, LANES), F32),
                        pltpu.VMEM((tm, d), F32)],
        compiler_params=_params("parallel", "arbitrary"),
        name="moe",
    )(x, g_ffn.reshape(1, d), wr, br, w1_bf, w3_bf, w2_bf, g_final.reshape(1, d))


def _page_gate_kernel(pt_ref, *refs, pages_per_step, n_blocks):
    page_refs = refs[:pages_per_step]
    q_ref, kn_ref, top_ref, ksum_sc = refs[pages_per_step:]
    c = pl.program_id(1)
    width = q_ref.shape[-1]
    n_heads = width // HEAD_DIM
    per_blk = MOBA_BLOCK // PAGE_SIZE
    n_cols = pages_per_step // per_blk
    lane = lax.broadcasted_iota(jnp.int32, (1, LANES), 1)

    @pl.when(c == 0)
    def _():
        ksum_sc[...] = jnp.zeros_like(ksum_sc)

    upd = ksum_sc[...]
    for r in range(n_cols):
        pages = [page_refs[r * per_blk + g][...] for g in range(per_blk)]
        col = jnp.sum(functools.reduce(lambda a, b: a + b, pages), axis=1, keepdims=True)
        upd = jnp.where(lane == c * n_cols + r, col, upd)
    ksum_sc[...] = upd

    @pl.when(c == pl.num_programs(1) - 1)
    def _():
        hrow = lax.broadcasted_iota(jnp.int32, (n_heads, width), 0)
        feat = lax.broadcasted_iota(jnp.int32, (n_heads, width), 1)
        qblk = jnp.where(_idiv(feat, HEAD_DIM) == hrow, q_ref[...], 0.0)
        blk = lax.broadcasted_iota(jnp.int32, (n_heads, LANES), 1)
        gate_past = _dot_exact(qblk, ksum_sc[...] / MOBA_BLOCK)
        gate_own = jnp.sum(qblk * (kn_ref[...] / MOBA_BLOCK), axis=1, keepdims=True)
        gate = jnp.where(blk == n_blocks, gate_own, gate_past)
        gate = jnp.where(blk < n_blocks, gate, -jnp.inf)
        slot = lax.broadcasted_iota(jnp.int32, (n_heads, MOBA_TOPK), 1)
        top = jnp.zeros((n_heads, MOBA_TOPK), jnp.int32)
        for t in range(MOBA_TOPK):
            gmax = jnp.max(gate, axis=1, keepdims=True)
            first = jnp.min(jnp.where(gate == gmax, blk, LANES), axis=1, keepdims=True)
            top = jnp.where(slot == t, first, top)
            gate = jnp.where(blk == first, -jnp.inf, gate)
        top_ref[...] = top


def _page_gate(cache_kt, page_table, q_s, k_new):
    bd, n_pages = page_table.shape
    width = q_s.shape[-1]
    n_blocks = n_pages * PAGE_SIZE // MOBA_BLOCK
    assert n_blocks < LANES
    pages_per_step = 16
    assert n_pages % pages_per_step == 0
    steps = n_pages // pages_per_step
    n_heads = width // HEAD_DIM

    def page_spec(j):
        return pl.BlockSpec(
            (None, width, PAGE_SIZE),
            lambda b, c, pt: (pt[b * n_pages + c * pages_per_step + j], 0, 0))

    row = pl.BlockSpec((None, 1, width), lambda b, c, pt: (b, 0, 0))
    grid_spec = pltpu.PrefetchScalarGridSpec(
        num_scalar_prefetch=1,
        grid=(bd, steps),
        in_specs=[page_spec(j) for j in range(pages_per_step)] + [row, row],
        out_specs=pl.BlockSpec((None, n_heads, MOBA_TOPK), lambda b, c, pt: (b, 0, 0)),
        scratch_shapes=[pltpu.VMEM((width, LANES), F32)],
    )
    kern = functools.partial(_page_gate_kernel, pages_per_step=pages_per_step, n_blocks=n_blocks)
    return pl.pallas_call(
        kern,
        grid_spec=grid_spec,
        out_shape=jax.ShapeDtypeStruct((bd, n_heads, MOBA_TOPK), jnp.int32),
        compiler_params=_params("parallel", "arbitrary"),
        name="page_gate",
    )(page_table.reshape(-1), *([cache_kt] * pages_per_step), q_s.reshape(bd, 1, width),
      k_new.reshape(bd, 1, width))


def _moba_sample_kernel(pt_ref, top_ref, *refs, n_sel_pages):
    k_refs = refs[:2 * n_sel_pages]
    v_refs = refs[2 * n_sel_pages:4 * n_sel_pages]
    q_ref, kn_ref, vn_ref, o_ref = refs[4 * n_sel_pages:]
    q = q_ref[...]
    lane = lax.broadcasted_iota(jnp.int32, (1, LANES), 1)
    diag = _diag_mask(LANES)
    qcol = jnp.sum(jnp.where(diag, q, 0.0), axis=1, keepdims=True)
    ocols, p_owns, dens = [], [], []
    for hd in range(2):
        qc = qcol[hd * HEAD_DIM:(hd + 1) * HEAD_DIM]
        in_head = _idiv(lane, HEAD_DIM) == hd
        ks = k_refs[hd * n_sel_pages:(hd + 1) * n_sel_pages]
        vs = v_refs[hd * n_sel_pages:(hd + 1) * n_sel_pages]
        s_own = jnp.sum(jnp.where(in_head, q * kn_ref[...], 0.0), axis=1, keepdims=True)
        scores = [jnp.sum(r[...] * qc, axis=0, keepdims=True) for r in ks]
        m = s_own
        for s in scores:
            m = jnp.maximum(m, jnp.max(s, axis=1, keepdims=True))
        p_own = jnp.exp(s_own - m)
        den = p_own
        acc = jnp.zeros((HEAD_DIM, PAGE_SIZE), F32)
        for s, vr in zip(scores, vs):
            p = jnp.exp(s - m)
            den = den + jnp.sum(p, axis=1, keepdims=True)
            acc = acc + vr[...] * p
        ocols.append(jnp.sum(acc, axis=1, keepdims=True))
        p_owns.append(p_own)
        dens.append(den)
    ocol = jnp.concatenate(ocols, axis=0)
    orow = jnp.sum(jnp.where(diag, ocol, 0.0), axis=0, keepdims=True)
    first = lane < HEAD_DIM
    p_own = jnp.where(first, p_owns[0], p_owns[1])
    den = jnp.where(first, dens[0], dens[1])
    o_ref[...] = (orow + p_own * vn_ref[...]) / den


def _moba_sample(cache_kt, cache_vt, page_table, top, q_s, k_new, v_new):
    bd, n_pages = page_table.shape
    width = q_s.shape[-1]
    n_pairs = width // LANES
    per_blk = MOBA_BLOCK // PAGE_SIZE
    n_sel_pages = MOBA_TOPK * per_blk
    n_heads = width // HEAD_DIM

    def page_spec(hd, t, pg):
        def index(b, hp, pt, tp):
            head = 2 * hp + hd
            blk = tp[(b * n_heads + head) * MOBA_TOPK + t]
            return (pt[b * n_pages + blk * per_blk + pg], head, 0)
        return pl.BlockSpec((None, HEAD_DIM, PAGE_SIZE), index)

    page_specs = [page_spec(hd, t, pg) for hd in range(2) for t in range(MOBA_TOPK)
                  for pg in range(per_blk)]
    row = pl.BlockSpec((None, None, 1, LANES), lambda b, hp, pt, tp: (b, hp, 0, 0))
    grid_spec = pltpu.PrefetchScalarGridSpec(
        num_scalar_prefetch=2,
        grid=(bd, n_pairs),
        in_specs=page_specs + page_specs + [row, row, row],
        out_specs=row,
    )
    kern = functools.partial(_moba_sample_kernel, n_sel_pages=n_sel_pages)
    as_rows = lambda t: t.reshape(bd, n_pairs, 1, LANES)
    n_in = len(page_specs)
    out = pl.pallas_call(
        kern,
        grid_spec=grid_spec,
        out_shape=jax.ShapeDtypeStruct((bd, n_pairs, 1, LANES), F32),
        compiler_params=_params("parallel", "parallel"),
        name="moba_sample",
    )(page_table.reshape(-1), top.reshape(-1), *([cache_kt] * n_in), *([cache_vt] * n_in),
      as_rows(q_s), as_rows(k_new), as_rows(v_new))
    return out.reshape(bd, width)


def _mix_sample_kernel(x_ref, u_ref, st_ref, oa_ref, wp_ref, sp_ref, wout_ref, x1_ref):
    width = u_ref.shape[-1]
    gw = width // len(POOL_WINDOWS)
    u = u_ref[...]
    groups = []
    for g, w in enumerate(POOL_WINDOWS):
        sl = slice(g * gw, (g + 1) * gw)
        a = u[:, sl]
        for back in range(1, w):
            a = a + st_ref[POOL_BUF - back][:, sl]
        groups.append(a / float(w) - u[:, sl])
    z = _pool_mix(groups, wp_ref, sp_ref).astype(BF16)
    x1_ref[...] = (x_ref[...] + _dot(z, wout_ref[:width, :])
                   + _dot(oa_ref[...].astype(BF16), wout_ref[width:, :]))


def _mix_sample(x, u, state_t, o_attn, wp_bf, s_pool, wout_bf):
    t, d = x.shape
    width = u.shape[-1]
    return pl.pallas_call(
        _mix_sample_kernel,
        out_shape=jax.ShapeDtypeStruct((t, d), F32),
        compiler_params=pltpu.CompilerParams(vmem_limit_bytes=VMEM_LIMIT),
        name="mix_sample",
    )(x, u, state_t, o_attn, wp_bf, s_pool.reshape(1, width), wout_bf)


def _mem_sample_kernel(x1_ref, gm_ref, wq_ref, mk_ref, mv_ref, wo_ref, x2_ref):
    x1 = x1_ref[...]
    n = x1.shape[0]
    hm = _rms(x1, gm_ref[...]).astype(BF16)
    qm = _dot(hm, wq_ref[...])
    rows = []
    for r in range(n):
        outs = []
        for hh in range(qm.shape[1] // MEM_HEAD_DIM):
            sl = slice(hh * MEM_HEAD_DIM, (hh + 1) * MEM_HEAD_DIM)
            kh = mk_ref[r][:, sl]
            vh = mv_ref[r][:, sl]
            s = jnp.sum(kh * qm[r:r + 1, sl], axis=1, keepdims=True) * (MEM_HEAD_DIM ** -0.5)
            p = jnp.exp(s - jnp.max(s, axis=0, keepdims=True))
            p = p / jnp.sum(p, axis=0, keepdims=True)
            outs.append(jnp.sum(p * vh, axis=0, keepdims=True))
        rows.append(jnp.concatenate(outs, axis=1))
    o = jnp.concatenate(rows, axis=0).astype(BF16)
    x2_ref[...] = x1 + _dot(o, wo_ref[...])


def _mem_sample(x1, g_mem, wq_bf, mem_k, mem_v, wo_bf):
    t, d = x1.shape
    m, mw = mem_k.shape[1:]
    n = 8
    tok = pl.BlockSpec((n, d), lambda i: (i, 0))
    mem = pl.BlockSpec((n, m, mw), lambda i: (i, 0, 0))
    return pl.pallas_call(
        _mem_sample_kernel,
        grid=(t // n,),
        in_specs=[tok, _const_spec((1, d)), _const_spec(wq_bf.shape), mem, mem,
                  _const_spec(wo_bf.shape)],
        out_specs=tok,
        out_shape=jax.ShapeDtypeStruct((t, d), F32),
        compiler_params=_params("parallel"),
        name="mem_sample",
    )(x1, g_mem.reshape(1, d), wq_bf, mem_k, mem_v, wo_bf)


def kernel(x_prompt, x_sample, mem_prompt, cache_k, cache_v, cache_mem_k, cache_mem_v, state_pool,
           page_table, g_mix, w_in, w_pool, s_pool, w_out, g_mem, g_mem_in, w_mq, w_mk, w_mv, w_mo,
           g_ffn, w_rg, b_rg, w_re, b_re, w1, w3, w2, g_final):
    assert g_mix.shape[0] == 1, "one layer"
    b, s, d = x_prompt.shape
    bd, n_new, _ = x_sample.shape
    assert n_new == 1 and s % MOBA_BLOCK == 0
    n_pages = page_table.shape[1]
    past_len = n_pages * PAGE_SIZE
    assert past_len % MOBA_BLOCK == 0 and past_len // MOBA_BLOCK >= MOBA_TOPK
    width = w_in.shape[2] // 4
    n_exp = w1.shape[1]
    n_heads = width // HEAD_DIM

    wuq_bf = w_in[0][:, :2 * width].astype(BF16)
    wkv_t_bf = w_in[0][:, 2 * width:].T.astype(BF16)
    wp_bf = w_pool[0].astype(BF16)
    wout_bf = w_out[0].astype(BF16)
    wq_bf = w_mq[0].astype(BF16)
    wmkv_bf = jnp.concatenate([w_mk[0], w_mv[0]], axis=1).astype(BF16)
    wo_bf = w_mo[0].astype(BF16)
    w1_bf, w3_bf, w2_bf = w1[0].astype(BF16), w3[0].astype(BF16), w2[0].astype(BF16)
    n_grp = w_rg.shape[2]
    wr = jnp.zeros((d, LANES), F32).at[:, :n_exp].set(w_re[0]).at[:, n_exp:n_exp + n_grp].set(w_rg[0])
    br = jnp.zeros((1, LANES), F32).at[0, :n_exp].set(b_re[0]).at[0, n_exp:n_exp + n_grp].set(b_rg[0])

    mk_p, mv_p = _memkv(mem_prompt, g_mem_in[0], wmkv_bf)
    pos_p = jnp.arange(s, dtype=jnp.int32)
    u_p, q_p, kt_p, vt_p, ktb_p, vtb_p = _inproj(x_prompt, pos_p, g_mix[0], wuq_bf, wkv_t_bf,
                                                 min(s, 512), True)
    oa_p = _moba_prompt(q_p, ktb_p, vtb_p)
    x2_p = _mix_prompt(x_prompt, u_p, oa_p, wp_bf, s_pool[0], wout_bf, g_mem[0], wq_bf,
                       mk_p, mv_p, wo_bf, min(s, 512))
    y_p = _moe(x2_p.reshape(b * s, d), g_ffn[0], wr, br, w1_bf, w3_bf, w2_bf, g_final,
               min(b * s, 1024)).reshape(b, s, d)

    pos_s = jnp.full((bd,), past_len, dtype=jnp.int32)
    u_s, q_s, kt_s, vt_s = _inproj(x_sample.reshape(1, bd, d), pos_s, g_mix[0], wuq_bf, wkv_t_bf,
                                   bd, False)
    u_s, q_s = u_s.reshape(bd, width), q_s.reshape(bd, width)
    k_s, v_s = kt_s[0].T, vt_s[0].T
    ckt = jnp.transpose(cache_k[0], (0, 2, 3, 1)).reshape(cache_k.shape[1], width, PAGE_SIZE)
    cvt = jnp.transpose(cache_v[0], (0, 2, 3, 1)).reshape(cache_v.shape[1], width, PAGE_SIZE)
    top = _page_gate(ckt, page_table, q_s, k_s)
    oa_s = _moba_sample(ckt, cvt, page_table, top, q_s, k_s, v_s)
    state_t = jnp.transpose(state_pool[0], (1, 0, 2))
    x1_s = _mix_sample(x_sample.reshape(bd, d), u_s, state_t, oa_s, wp_bf, s_pool[0], wout_bf)
    mem_w = cache_mem_k.shape[3] * cache_mem_k.shape[4]
    x2_s = _mem_sample(x1_s, g_mem[0], wq_bf, cache_mem_k[0].reshape(bd, -1, mem_w),
                       cache_mem_v[0].reshape(bd, -1, mem_w), wo_bf)
    y_s = _moe(x2_s, g_ffn[0], wr, br, w1_bf, w3_bf, w2_bf, g_final, bd).reshape(bd, 1, d)

    def heads(t_feat, n):
        lead = t_feat.shape[0]
        return jnp.transpose(t_feat.reshape(lead, n_heads, HEAD_DIM, n), (0, 3, 1, 2))[None]

    mem_heads = lambda t: t.reshape(1, b, t.shape[1], -1, MEM_HEAD_DIM)
    new_pool_p = u_p[:, s - POOL_BUF:][None]
    new_pool_s = jnp.concatenate([state_pool[0][:, 1:], u_s[:, None, :]], axis=1)[None]
    new_k_s = jnp.transpose(heads(kt_s, bd), (0, 2, 1, 3, 4))
    new_v_s = jnp.transpose(heads(vt_s, bd), (0, 2, 1, 3, 4))
    return (y_p, y_s, heads(kt_p, s), heads(vt_p, s), new_pool_p, mem_heads(mk_p),
            mem_heads(mv_p), new_k_s, new_v_s, new_pool_s)
```

```python
import functools

import jax
import jax.numpy as jnp
from jax import lax
from jax.experimental import pallas as pl
from jax.experimental.pallas import tpu as pltpu

F32 = jnp.float32
BF16 = jnp.bfloat16

POOL_WINDOWS = (2, 4, 8, 16)
POOL_BUF = max(POOL_WINDOWS) - 1
HEAD_DIM = 64
ROPE_DIM = HEAD_DIM // 4
ROPE_THETA = 500000.0
MOBA_BLOCK = 256
MOBA_TOPK = 3
MOBA_GROUP = 4
PAGE_SIZE = 128
MEM_HEAD_DIM = 128
N_GROUPS = 4
EXPERTS_PER_GROUP = 8
MOE_EXPERTS_PER_STEP = 4
RMS_EPS = 1e-6
NEG_INF = -1e30
LOG2_E = 1.4426950408889634

LANES = 128
VMEM_LIMIT = 56 * 1024 * 1024


def _params(*sem):
    return pltpu.CompilerParams(dimension_semantics=sem, vmem_limit_bytes=VMEM_LIMIT)


def _rms(x, g):
    ms = jnp.mean(x * x, axis=-1, keepdims=True)
    return x * lax.rsqrt(ms + RMS_EPS) * g


def _dot(a, b):
    return jnp.dot(a, b, preferred_element_type=F32)


def _dot_nt(a, b):
    return lax.dot_general(a, b, (((1,), (1,)), ((), ())), preferred_element_type=F32)


def _dot_exact(a, b):
    return jnp.dot(a, b, preferred_element_type=F32, precision=lax.Precision.HIGHEST)


def _idiv(x, n):
    shift = n.bit_length() - 1
    assert 1 << shift == n
    return jnp.right_shift(x, shift)


def _const_spec(shape):
    return pl.BlockSpec(shape, lambda *_: (0,) * len(shape))


def _diag_mask(n):
    return (lax.broadcasted_iota(jnp.int32, (n, n), 0) == lax.broadcasted_iota(jnp.int32, (n, n), 1))


def _inproj_kernel(x_ref, g_ref, wtok_ref, wfeat_ref, cos_ref, sa_ref, sb_ref, cos_t_ref,
                   sa_t_ref, sb_t_ref, u_ref, qt_ref, kt_ref, vt_ref, *attn_refs):
    width = u_ref.shape[-1]
    half = ROPE_DIM // 2
    n_pairs = width // LANES
    h = _rms(x_ref[...], g_ref[...]).astype(BF16)

    def rope_feat(t):
        outs = []
        for c in range(n_pairs):
            tc = t[c * LANES:(c + 1) * LANES, :]
            outs.append(tc * cos_t_ref[...] + pltpu.roll(tc, LANES - half, axis=0) * sa_t_ref[...]
                        + pltpu.roll(tc, half, axis=0) * sb_t_ref[...])
        return jnp.concatenate(outs, axis=0)

    def rope_tok(t):
        outs = []
        for c in range(n_pairs):
            tc = t[:, c * LANES:(c + 1) * LANES]
            outs.append(tc * cos_ref[...] + pltpu.roll(tc, LANES - half, axis=1) * sa_ref[...]
                        + pltpu.roll(tc, half, axis=1) * sb_ref[...])
        return jnp.concatenate(outs, axis=1)

    feat = _dot_nt(wfeat_ref[...], h)
    qt_ref[...] = rope_feat(feat[:width]) * (HEAD_DIM ** -0.5)
    kt_ref[...] = rope_feat(feat[width:2 * width])
    vt = feat[2 * width:]
    vt_ref[...] = vt
    if attn_refs:
        kb_ref, vtb_ref = attn_refs
        tok = _dot(h, wtok_ref[...])
        u_ref[...] = tok[:, :width]
        kb_ref[...] = rope_tok(tok[:, width:]).astype(BF16)
        for jj in range(vtb_ref.shape[0]):
            vtb_ref[jj] = vt[:, jj * MOBA_BLOCK:(jj + 1) * MOBA_BLOCK].astype(BF16)
    else:
        u_ref[...] = _dot(h, wtok_ref[:, :width])


def _rope_tables(pos):
    half = ROPE_DIM // 2
    inv_freq = ROPE_THETA ** (-jnp.arange(half, dtype=F32) / half)
    ang = pos.astype(F32)[:, None] * inv_freq[None, :]
    cos, sin = jnp.cos(ang), jnp.sin(ang)
    n = pos.shape[0]
    pad = jnp.zeros((n, HEAD_DIM - ROPE_DIM), F32)
    cos_h = jnp.concatenate([cos, cos, pad + 1.0], axis=1)
    sa_h = jnp.concatenate([-sin, jnp.zeros_like(sin), pad], axis=1)
    sb_h = jnp.concatenate([jnp.zeros_like(sin), sin, pad], axis=1)
    rep = LANES // HEAD_DIM
    return tuple(jnp.tile(t, (1, rep)) for t in (cos_h, sa_h, sb_h))


def _inproj(x, pos, g_mix, wtok_bf, wfeat_bf, tm, attn_copies):
    b, s, d = x.shape
    width = wtok_bf.shape[1] // 2
    tabs = _rope_tables(pos)
    tabs_t = tuple(t.T for t in tabs)
    tok = lambda last: pl.BlockSpec((None, tm, last), lambda bi, i: (bi, i, 0))
    feat = pl.BlockSpec((None, width, tm), lambda bi, i: (bi, 0, i))
    tab = pl.BlockSpec((tm, LANES), lambda bi, i: (i, 0))
    tab_t = pl.BlockSpec((LANES, tm), lambda bi, i: (0, i))
    feat_shape = jax.ShapeDtypeStruct((b, width, s), F32)
    out_specs = [tok(width), feat, feat, feat]
    out_shape = [jax.ShapeDtypeStruct((b, s, width), F32), feat_shape, feat_shape, feat_shape]
    if attn_copies:
        per = tm // MOBA_BLOCK
        out_specs += [tok(width),
                      pl.BlockSpec((None, per, width, MOBA_BLOCK), lambda bi, i: (bi, i, 0, 0))]
        out_shape += [jax.ShapeDtypeStruct((b, s, width), BF16),
                      jax.ShapeDtypeStruct((b, s // MOBA_BLOCK, width, MOBA_BLOCK), BF16)]
    return pl.pallas_call(
        _inproj_kernel,
        grid=(b, s // tm),
        in_specs=[tok(d), _const_spec((1, d)), _const_spec(wtok_bf.shape),
                  _const_spec(wfeat_bf.shape), tab, tab, tab, tab_t, tab_t, tab_t],
        out_specs=out_specs,
        out_shape=out_shape,
        compiler_params=_params("parallel", "parallel"),
        name="inproj",
    )(x, g_mix.reshape(1, d), wtok_bf, wfeat_bf, *tabs, *tabs_t)


def _moba_prompt_kernel(qt_ref, k_ref, vt_ref, vo_ref, o_ref, kmean_sc):
    i = pl.program_id(2)
    nb = vt_ref.shape[0]
    blk = qt_ref.shape[1]

    @pl.when(i == 0)
    def _():
        r = lax.broadcasted_iota(jnp.int32, (nb, nb * blk), 1)
        j = lax.broadcasted_iota(jnp.int32, (nb, nb * blk), 0)
        onehot = jnp.where(_idiv(r, blk) == j, 1.0, 0.0).astype(BF16)
        kmean_sc[...] = _dot(onehot, k_ref[...]) / blk

    qt = qt_ref[...] * LOG2_E
    feat = lax.broadcasted_iota(jnp.int32, (LANES, 1), 0)
    qbd = jnp.concatenate([jnp.where(_idiv(feat, HEAD_DIM) == hd, qt, 0.0) for hd in range(2)],
                          axis=1)
    qb = qbd.astype(BF16)
    blk_id = lax.broadcasted_iota(jnp.int32, (nb, 2 * blk), 0)
    gate = jnp.where(blk_id < i, _dot_exact(kmean_sc[...], qbd), -jnp.inf)
    sel = jnp.zeros((nb, 2 * blk), F32)
    for _ in range(MOBA_TOPK):
        gmax = jnp.max(gate, axis=0, keepdims=True)
        first = jnp.min(jnp.where(gate == gmax, blk_id, nb), axis=0, keepdims=True)
        pick = blk_id == first
        sel = jnp.where(pick, 1.0, sel)
        gate = jnp.where(pick, -jnp.inf, gate)
    sel = jnp.where(blk_id < i, sel, 0.0)

    def attend(st, vts, carry):
        m, l, acc = carry
        m_new = jnp.maximum(m, jnp.max(st, axis=0, keepdims=True))
        alpha = jnp.exp2(m - m_new)
        p = jnp.exp2(st - m_new)
        l_new = alpha * l + jnp.sum(p, axis=0, keepdims=True)
        pb = p.astype(BF16)
        acc = alpha * acc
        for g, vt in enumerate(vts):
            acc = acc + _dot(vt, pb[g * blk:(g + 1) * blk])
        return m_new, l_new, acc

    carry = (jnp.full((1, 2 * blk), NEG_INF, F32), jnp.zeros((1, 2 * blk), F32),
             jnp.zeros((LANES, 2 * blk), F32))
    key_id = lax.broadcasted_iota(jnp.int32, (blk, 2 * blk), 0)
    qry_id = lax.broadcasted_iota(jnp.int32, (blk, 2 * blk), 1) & (blk - 1)
    own0 = pl.multiple_of(i * blk, blk)
    s_own = jnp.where(key_id <= qry_id, _dot(k_ref[pl.ds(own0, blk), :], qb), NEG_INF)
    carry = attend(s_own, [vo_ref[...].astype(BF16)], carry)

    def body(t, carry):
        r0 = pl.multiple_of(t * (MOBA_GROUP * blk), MOBA_GROUP * blk)
        s = _dot(k_ref[pl.ds(r0, MOBA_GROUP * blk), :], qb)
        segs, vts = [], []
        for g in range(MOBA_GROUP):
            jb = t * MOBA_GROUP + g
            chosen = jnp.sum(jnp.where(blk_id == jb, sel, 0.0), axis=0, keepdims=True) > 0.5
            segs.append(jnp.where(chosen, s[g * blk:(g + 1) * blk], NEG_INF))
            vts.append(vt_ref[jb])
        return attend(jnp.concatenate(segs, axis=0), vts, carry)

    n_groups = lax.shift_right_logical(i + (MOBA_GROUP - 1), MOBA_GROUP.bit_length() - 1)
    m, l, acc = lax.fori_loop(0, n_groups, body, carry)
    ot = jnp.concatenate([acc[hd * HEAD_DIM:(hd + 1) * HEAD_DIM, hd * blk:(hd + 1) * blk]
                          / l[:, hd * blk:(hd + 1) * blk] for hd in range(2)], axis=0)
    o_ref[...] = ot.T.astype(o_ref.dtype)


def _moba_prompt(qt, kb, vtb, vt):
    b, width, s = qt.shape
    nb = s // MOBA_BLOCK
    assert nb % MOBA_GROUP == 0
    blk = MOBA_BLOCK
    qspec = pl.BlockSpec((None, LANES, blk), lambda bi, hp, i: (bi, hp, i))
    kspec = pl.BlockSpec((None, s, LANES), lambda bi, hp, i: (bi, 0, hp))
    vspec = pl.BlockSpec((None, nb, LANES, blk), lambda bi, hp, i: (bi, 0, hp, 0))
    ospec = pl.BlockSpec((None, blk, LANES), lambda bi, hp, i: (bi, i, hp))
    return pl.pallas_call(
        _moba_prompt_kernel,
        grid=(b, width // LANES, nb),
        in_specs=[qspec, kspec, vspec, qspec],
        out_specs=ospec,
        out_shape=jax.ShapeDtypeStruct((b, s, width), BF16),
        scratch_shapes=[pltpu.VMEM((nb, LANES), F32)],
        compiler_params=_params("parallel", "parallel", "arbitrary"),
        name="moba_prompt",
    )(qt, kb, vtb, vt)


def _memkv_kernel(mem_ref, g_ref, w_ref, mk_ref, mv_ref):
    width = mk_ref.shape[-1]
    hm = _rms(mem_ref[...], g_ref[...]).astype(BF16)
    kv = _dot(hm, w_ref[...])
    mk_ref[...] = kv[:, :width]
    mv_ref[...] = kv[:, width:]


def _memkv(mem, g_mem_in, w_mkv_bf):
    b, m, d = mem.shape
    width = w_mkv_bf.shape[1] // 2
    out = jax.ShapeDtypeStruct((b, m, width), F32)
    return pl.pallas_call(
        _memkv_kernel,
        grid=(b,),
        in_specs=[pl.BlockSpec((None, m, d), lambda bi: (bi, 0, 0)), _const_spec((1, d)),
                  _const_spec(w_mkv_bf.shape)],
        out_specs=[pl.BlockSpec((None, m, width), lambda bi: (bi, 0, 0))] * 2,
        out_shape=[out, out],
        compiler_params=_params("parallel"),
        name="memkv",
    )(mem, g_mem_in.reshape(1, d), w_mkv_bf)


def _pool_mix(z_in_groups, wp_ref, sp_ref):
    zs = [_dot(d.astype(BF16), wp_ref[g]) for g, d in enumerate(z_in_groups)]
    return jnp.concatenate(zs, axis=1) * sp_ref[...]


def _cross_attend(x1, gm_ref, wq_ref, mk, mv, wo_ref):
    hm = _rms(x1, gm_ref[...]).astype(BF16)
    qm = _dot(hm, wq_ref[...])
    outs = []
    for hh in range(qm.shape[1] // MEM_HEAD_DIM):
        sl = slice(hh * MEM_HEAD_DIM, (hh + 1) * MEM_HEAD_DIM)
        s = _dot_nt(qm[:, sl].astype(BF16), mk[:, sl]) * (MEM_HEAD_DIM ** -0.5)
        p = jnp.exp(s - jnp.max(s, axis=1, keepdims=True))
        p = p / jnp.sum(p, axis=1, keepdims=True)
        outs.append(_dot(p.astype(BF16), mv[:, sl]))
    o = jnp.concatenate(outs, axis=1).astype(BF16)
    return x1 + _dot(o, wo_ref[...])


def _mix_prompt_kernel(x_ref, u_ref, halo_ref, oa_ref, wp_ref, sp_ref, wout_ref, gm_ref, wq_ref,
                       mk_ref, mv_ref, wo_ref, x2_ref):
    i = pl.program_id(1)
    tm, width = u_ref.shape
    gw = width // len(POOL_WINDOWS)
    halo_rows = halo_ref.shape[0]
    u = u_ref[...]
    halo = jnp.where(i > 0, halo_ref[...], 0.0)
    ext = jnp.concatenate([halo, u], axis=0)
    pos = i * tm + lax.broadcasted_iota(jnp.int32, (tm, 1), 0)
    groups = []
    for g, w in enumerate(POOL_WINDOWS):
        a = ext[:, g * gw:(g + 1) * gw]
        span = 1
        while span < w:
            a = a + pltpu.roll(a, span, axis=0)
            span *= 2
        cnt = jnp.minimum(w, pos + 1).astype(F32)
        groups.append(a[halo_rows:] / cnt - u[:, g * gw:(g + 1) * gw])
    z = _pool_mix(groups, wp_ref, sp_ref).astype(BF16)
    x1 = x_ref[...] + _dot(z, wout_ref[:width, :]) + _dot(oa_ref[...], wout_ref[width:, :])
    x2_ref[...] = _cross_attend(x1, gm_ref, wq_ref, mk_ref[...].astype(BF16),
                                mv_ref[...].astype(BF16), wo_ref)


def _mix_prompt(x, u, o_attn, wp_bf, s_pool, wout_bf, g_mem, wq_bf, mk, mv, wo_bf, tm):
    b, s, d = x.shape
    width = u.shape[-1]
    m = mk.shape[1]
    halo = 16
    tok = lambda last: pl.BlockSpec((None, tm, last), lambda bi, i: (bi, i, 0))
    halo_spec = pl.BlockSpec((None, halo, width),
                             lambda bi, i: (bi, jnp.maximum(i * (tm // halo) - 1, 0), 0))
    mem_spec = pl.BlockSpec((None, m, mk.shape[-1]), lambda bi, i: (bi, 0, 0))
    return pl.pallas_call(
        _mix_prompt_kernel,
        grid=(b, s // tm),
        in_specs=[tok(d), tok(width), halo_spec, tok(width), _const_spec(wp_bf.shape),
                  _const_spec((1, width)), _const_spec(wout_bf.shape), _const_spec((1, d)),
                  _const_spec(wq_bf.shape), mem_spec, mem_spec, _const_spec(wo_bf.shape)],
        out_specs=tok(d),
        out_shape=jax.ShapeDtypeStruct((b, s, d), F32),
        compiler_params=_params("parallel", "parallel"),
        name="mix_prompt",
    )(x, u, u, o_attn, wp_bf, s_pool.reshape(1, width), wout_bf, g_mem.reshape(1, d), wq_bf,
      mk, mv, wo_bf)


def _moe_kernel(x_ref, gf_ref, wr_ref, br_ref, w1_ref, w3_ref, w2_ref, gfin_ref, y_ref,
                h_sc, comb_sc, acc_sc):
    e = pl.program_id(1)
    n_exp = pl.num_programs(1)
    tm = x_ref.shape[0]
    lane = lax.broadcasted_iota(jnp.int32, (tm, LANES), 1)

    @pl.when(e == 0)
    def _():
        t = _rms(x_ref[...], gf_ref[...])
        h_sc[...] = t.astype(BF16)
        logits = _dot_exact(t, wr_ref[...]) + br_ref[...]
        n_e = N_GROUPS * EXPERTS_PER_GROUP
        is_g = (lane >= n_e) & (lane < n_e + N_GROUPS)
        lg = jnp.where(is_g, logits, -jnp.inf)
        gmax = jnp.max(lg, axis=1, keepdims=True)
        grp = jnp.min(jnp.where(lg == gmax, lane, LANES), axis=1, keepdims=True) - n_e
        p_grp = 1.0 / jnp.sum(jnp.where(is_g, jnp.exp(logits - gmax), 0.0), axis=1, keepdims=True)
        le = jnp.where(_idiv(lane, EXPERTS_PER_GROUP) == grp, logits, -jnp.inf)
        t1 = jnp.max(le, axis=1, keepdims=True)
        i1 = jnp.min(jnp.where(le == t1, lane, LANES), axis=1, keepdims=True)
        le2 = jnp.where(lane == i1, -jnp.inf, le)
        t2 = jnp.max(le2, axis=1, keepdims=True)
        i2 = jnp.min(jnp.where(le2 == t2, lane, LANES), axis=1, keepdims=True)
        r = jnp.exp(t2 - t1)
        g1 = 1.0 / (1.0 + r)
        g2 = r / (1.0 + r)
        comb_sc[...] = p_grp * (jnp.where(lane == i1, g1, 0.0) + jnp.where(lane == i2, g2, 0.0))
        acc_sc[...] = jnp.zeros_like(acc_sc)

    h = h_sc[...]
    a1 = _dot(h, w1_ref[...])
    a3 = _dot(h, w3_ref[...])
    a = (a1 * (1.0 / (1.0 + jnp.exp(-a1)))) * a3
    hid = a.shape[1] // MOE_EXPERTS_PER_STEP
    comb = comb_sc[...]
    scaled = []
    for k in range(MOE_EXPERTS_PER_STEP):
        c = jnp.sum(jnp.where(lane == e * MOE_EXPERTS_PER_STEP + k, comb, 0.0), axis=1, keepdims=True)
        scaled.append((a[:, k * hid:(k + 1) * hid] * c).astype(BF16))
    acc_sc[...] += _dot(jnp.concatenate(scaled, axis=1), w2_ref[...])

    @pl.when(e == n_exp - 1)
    def _():
        y_ref[...] = _rms(x_ref[...] + acc_sc[...], gfin_ref[...])


def _expert_runs(w1, w3, w2):
    n_exp, d, hid = w1.shape
    run = MOE_EXPERTS_PER_STEP
    assert n_exp % run == 0

    def up(w):
        w = w.astype(BF16).reshape(n_exp // run, run, d, hid)
        return jnp.transpose(w, (0, 2, 1, 3)).reshape(n_exp // run, d, run * hid)

    return up(w1), up(w3), w2.astype(BF16).reshape(n_exp // run, run * hid, d)


def _moe(x, g_ffn, wr, br, w1_bf, w3_bf, w2_bf, g_final, tm):
    t, d = x.shape
    n_runs, _, run_hid = w1_bf.shape
    tok = pl.BlockSpec((tm, d), lambda i, e: (i, 0))
    return pl.pallas_call(
        _moe_kernel,
        grid=(t // tm, n_runs),
        in_specs=[tok, _const_spec((1, d)), _const_spec(wr.shape), _const_spec((1, LANES)),
                  pl.BlockSpec((None, d, run_hid), lambda i, e: (e, 0, 0)),
                  pl.BlockSpec((None, d, run_hid), lambda i, e: (e, 0, 0)),
                  pl.BlockSpec((None, run_hid, d), lambda i, e: (e, 0, 0)),
                  _const_spec((1, d))],
        out_specs=tok,
        out_shape=jax.ShapeDtypeStruct((t, d), F32),
        scratch_shapes=[pltpu.VMEM((tm, d), BF16), pltpu.VMEM((tm, LANES), F32),
                        pltpu.VMEM((tm, d), F32)],
        compiler_params=_params("parallel", "arbitrary"),
        name="moe",
    )(x, g_ffn.reshape(1, d), wr, br, w1_bf, w3_bf, w2_bf, g_final.reshape(1, d))


def _page_gate_kernel(pt_ref, *refs, pages_per_step, n_blocks):
    page_refs = refs[:pages_per_step]
    q_ref, kn_ref, top_ref, ksum_sc = refs[pages_per_step:]
    c = pl.program_id(1)
    width = q_ref.shape[-1]
    n_heads = width // HEAD_DIM
    per_blk = MOBA_BLOCK // PAGE_SIZE
    n_cols = pages_per_step // per_blk
    lane = lax.broadcasted_iota(jnp.int32, (1, LANES), 1)

    @pl.when(c == 0)
    def _():
        ksum_sc[...] = jnp.zeros_like(ksum_sc)

    upd = ksum_sc[...]
    for r in range(n_cols):
        pages = [page_refs[r * per_blk + g][...] for g in range(per_blk)]
        col = jnp.sum(functools.reduce(lambda a, b: a + b, pages), axis=1, keepdims=True)
        upd = jnp.where(lane == c * n_cols + r, col, upd)
    ksum_sc[...] = upd

    @pl.when(c == pl.num_programs(1) - 1)
    def _():
        hrow = lax.broadcasted_iota(jnp.int32, (n_heads, width), 0)
        feat = lax.broadcasted_iota(jnp.int32, (n_heads, width), 1)
        qblk = jnp.where(_idiv(feat, HEAD_DIM) == hrow, q_ref[...], 0.0)
        blk = lax.broadcasted_iota(jnp.int32, (n_heads, LANES), 1)
        gate_past = _dot_exact(qblk, ksum_sc[...] / MOBA_BLOCK)
        gate_own = jnp.sum(qblk * (kn_ref[...] / MOBA_BLOCK), axis=1, keepdims=True)
        gate = jnp.where(blk == n_blocks, gate_own, gate_past)
        gate = jnp.where(blk < n_blocks, gate, -jnp.inf)
        slot = lax.broadcasted_iota(jnp.int32, (n_heads, MOBA_TOPK), 1)
        top = jnp.zeros((n_heads, MOBA_TOPK), jnp.int32)
        for t in range(MOBA_TOPK):
            gmax = jnp.max(gate, axis=1, keepdims=True)
            first = jnp.min(jnp.where(gate == gmax, blk, LANES), axis=1, keepdims=True)
            top = jnp.where(slot == t, first, top)
            gate = jnp.where(blk == first, -jnp.inf, gate)
        top_ref[...] = top


def _page_gate(cache_kt, page_table, q_s, k_new):
    bd, n_pages = page_table.shape
    width = q_s.shape[-1]
    n_blocks = n_pages * PAGE_SIZE // MOBA_BLOCK
    assert n_blocks < LANES
    pages_per_step = min(32, n_pages)
    assert n_pages % pages_per_step == 0
    steps = n_pages // pages_per_step
    n_heads = width // HEAD_DIM

    def page_spec(j):
        return pl.BlockSpec(
            (None, width, PAGE_SIZE),
            lambda b, c, pt: (pt[b * n_pages + c * pages_per_step + j], 0, 0))

    row = pl.BlockSpec((None, 1, width), lambda b, c, pt: (b, 0, 0))
    grid_spec = pltpu.PrefetchScalarGridSpec(
        num_scalar_prefetch=1,
        grid=(bd, steps),
        in_specs=[page_spec(j) for j in range(pages_per_step)] + [row, row],
        out_specs=pl.BlockSpec((None, n_heads, MOBA_TOPK), lambda b, c, pt: (b, 0, 0)),
        scratch_shapes=[pltpu.VMEM((width, LANES), F32)],
    )
    kern = functools.partial(_page_gate_kernel, pages_per_step=pages_per_step, n_blocks=n_blocks)
    return pl.pallas_call(
        kern,
        grid_spec=grid_spec,
        out_shape=jax.ShapeDtypeStruct((bd, n_heads, MOBA_TOPK), jnp.int32),
        compiler_params=_params("parallel", "arbitrary"),
        name="page_gate",
    )(page_table.reshape(-1), *([cache_kt] * pages_per_step), q_s.reshape(bd, 1, width),
      k_new.reshape(bd, 1, width))


def _moba_sample_kernel(pt_ref, top_ref, *refs, n_sel_pages):
    k_refs = refs[:2 * n_sel_pages]
    v_refs = refs[2 * n_sel_pages:4 * n_sel_pages]
    q_ref, kn_ref, vn_ref, o_ref = refs[4 * n_sel_pages:]
    q = q_ref[...]
    lane = lax.broadcasted_iota(jnp.int32, (1, LANES), 1)
    diag = _diag_mask(LANES)
    qcol = jnp.sum(jnp.where(diag, q, 0.0), axis=1, keepdims=True)
    ocols, p_owns, dens = [], [], []
    for hd in range(2):
        qc = qcol[hd * HEAD_DIM:(hd + 1) * HEAD_DIM]
        in_head = _idiv(lane, HEAD_DIM) == hd
        ks = k_refs[hd * n_sel_pages:(hd + 1) * n_sel_pages]
        vs = v_refs[hd * n_sel_pages:(hd + 1) * n_sel_pages]
        s_own = jnp.sum(jnp.where(in_head, q * kn_ref[...], 0.0), axis=1, keepdims=True)
        scores = [jnp.sum(r[...] * qc, axis=0, keepdims=True) for r in ks]
        m = s_own
        for s in scores:
            m = jnp.maximum(m, jnp.max(s, axis=1, keepdims=True))
        p_own = jnp.exp(s_own - m)
        den = p_own
        acc = jnp.zeros((HEAD_DIM, PAGE_SIZE), F32)
        for s, vr in zip(scores, vs):
            p = jnp.exp(s - m)
            den = den + jnp.sum(p, axis=1, keepdims=True)
            acc = acc + vr[...] * p
        ocols.append(jnp.sum(acc, axis=1, keepdims=True))
        p_owns.append(p_own)
        dens.append(den)
    ocol = jnp.concatenate(ocols, axis=0)
    orow = jnp.sum(jnp.where(diag, ocol, 0.0), axis=0, keepdims=True)
    first = lane < HEAD_DIM
    p_own = jnp.where(first, p_owns[0], p_owns[1])
    den = jnp.where(first, dens[0], dens[1])
    o_ref[...] = (orow + p_own * vn_ref[...]) / den


def _moba_sample(cache_kt, cache_vt, page_table, top, q_s, k_new, v_new):
    bd, n_pages = page_table.shape
    width = q_s.shape[-1]
    n_pairs = width // LANES
    per_blk = MOBA_BLOCK // PAGE_SIZE
    n_sel_pages = MOBA_TOPK * per_blk
    n_heads = width // HEAD_DIM

    def page_spec(hd, t, pg):
        def index(b, hp, pt, tp):
            head = 2 * hp + hd
            blk = tp[(b * n_heads + head) * MOBA_TOPK + t]
            return (pt[b * n_pages + blk * per_blk + pg], head, 0)
        return pl.BlockSpec((None, HEAD_DIM, PAGE_SIZE), index)

    page_specs = [page_spec(hd, t, pg) for hd in range(2) for t in range(MOBA_TOPK)
                  for pg in range(per_blk)]
    row = pl.BlockSpec((None, None, 1, LANES), lambda b, hp, pt, tp: (b, hp, 0, 0))
    grid_spec = pltpu.PrefetchScalarGridSpec(
        num_scalar_prefetch=2,
        grid=(bd, n_pairs),
        in_specs=page_specs + page_specs + [row, row, row],
        out_specs=row,
    )
    kern = functools.partial(_moba_sample_kernel, n_sel_pages=n_sel_pages)
    as_rows = lambda t: t.reshape(bd, n_pairs, 1, LANES)
    n_in = len(page_specs)
    out = pl.pallas_call(
        kern,
        grid_spec=grid_spec,
        out_shape=jax.ShapeDtypeStruct((bd, n_pairs, 1, LANES), F32),
        compiler_params=_params("parallel", "parallel"),
        name="moba_sample",
    )(page_table.reshape(-1), top.reshape(-1), *([cache_kt] * n_in), *([cache_vt] * n_in),
      as_rows(q_s), as_rows(k_new), as_rows(v_new))
    return out.reshape(bd, width)


def _mix_sample_kernel(x_ref, u_ref, st_ref, oa_ref, wp_ref, sp_ref, wout_ref, x1_ref):
    width = u_ref.shape[-1]
    gw = width // len(POOL_WINDOWS)
    u = u_ref[...]
    groups = []
    for g, w in enumerate(POOL_WINDOWS):
        sl = slice(g * gw, (g + 1) * gw)
        a = u[:, sl]
        for back in range(1, w):
            a = a + st_ref[POOL_BUF - back][:, sl]
        groups.append(a / float(w) - u[:, sl])
    z = _pool_mix(groups, wp_ref, sp_ref).astype(BF16)
    x1_ref[...] = (x_ref[...] + _dot(z, wout_ref[:width, :])
                   + _dot(oa_ref[...].astype(BF16), wout_ref[width:, :]))


def _mix_sample(x, u, state_t, o_attn, wp_bf, s_pool, wout_bf):
    t, d = x.shape
    width = u.shape[-1]
    return pl.pallas_call(
        _mix_sample_kernel,
        out_shape=jax.ShapeDtypeStruct((t, d), F32),
        compiler_params=pltpu.CompilerParams(vmem_limit_bytes=VMEM_LIMIT),
        name="mix_sample",
    )(x, u, state_t, o_attn, wp_bf, s_pool.reshape(1, width), wout_bf)


def _mem_sample_kernel(x1_ref, gm_ref, wq_ref, mk_ref, mv_ref, wo_ref, x2_ref):
    x1 = x1_ref[...]
    n = x1.shape[0]
    n_mem_heads = mk_ref.shape[2]
    hm = _rms(x1, gm_ref[...]).astype(BF16)
    qm = _dot(hm, wq_ref[...])
    rows = []
    for r in range(n):
        outs = []
        for hh in range(n_mem_heads):
            sl = slice(hh * MEM_HEAD_DIM, (hh + 1) * MEM_HEAD_DIM)
            kh = mk_ref[r, :, hh, :]
            vh = mv_ref[r, :, hh, :]
            s = jnp.sum(kh * qm[r:r + 1, sl], axis=1, keepdims=True) * (MEM_HEAD_DIM ** -0.5)
            p = jnp.exp(s - jnp.max(s, axis=0, keepdims=True))
            p = p / jnp.sum(p, axis=0, keepdims=True)
            outs.append(jnp.sum(p * vh, axis=0, keepdims=True))
        rows.append(jnp.concatenate(outs, axis=1))
    o = jnp.concatenate(rows, axis=0).astype(BF16)
    x2_ref[...] = x1 + _dot(o, wo_ref[...])


def _mem_sample(x1, g_mem, wq_bf, mem_k, mem_v, wo_bf):
    t, d = x1.shape
    m, mh, md = mem_k.shape[1:]
    n = 8
    tok = pl.BlockSpec((n, d), lambda i: (i, 0))
    mem = pl.BlockSpec((n, m, mh, md), lambda i: (i, 0, 0, 0))
    return pl.pallas_call(
        _mem_sample_kernel,
        grid=(t // n,),
        in_specs=[tok, _const_spec((1, d)), _const_spec(wq_bf.shape), mem, mem,
                  _const_spec(wo_bf.shape)],
        out_specs=tok,
        out_shape=jax.ShapeDtypeStruct((t, d), F32),
        compiler_params=_params("parallel"),
        name="mem_sample",
    )(x1, g_mem.reshape(1, d), wq_bf, mem_k, mem_v, wo_bf)


def kernel(x_prompt, x_sample, mem_prompt, cache_k, cache_v, cache_mem_k, cache_mem_v, state_pool,
           page_table, g_mix, w_in, w_pool, s_pool, w_out, g_mem, g_mem_in, w_mq, w_mk, w_mv, w_mo,
           g_ffn, w_rg, b_rg, w_re, b_re, w1, w3, w2, g_final):
    assert g_mix.shape[0] == 1, "one layer"
    b, s, d = x_prompt.shape
    bd, n_new, _ = x_sample.shape
    assert n_new == 1 and s % MOBA_BLOCK == 0
    n_pages = page_table.shape[1]
    past_len = n_pages * PAGE_SIZE
    assert past_len % MOBA_BLOCK == 0 and past_len // MOBA_BLOCK >= MOBA_TOPK
    width = w_in.shape[2] // 4
    n_exp = w1.shape[1]
    n_heads = width // HEAD_DIM

    w_in0 = w_in[0]
    wtok_bf = jnp.concatenate([w_in0[:, :width], w_in0[:, 2 * width:3 * width]], axis=1).astype(BF16)
    wfeat_bf = w_in0[:, width:].T.astype(BF16)
    wp_bf = w_pool[0].astype(BF16)
    wout_bf = w_out[0].astype(BF16)
    wq_bf = w_mq[0].astype(BF16)
    wmkv_bf = jnp.concatenate([w_mk[0], w_mv[0]], axis=1).astype(BF16)
    wo_bf = w_mo[0].astype(BF16)
    w1_bf, w3_bf, w2_bf = _expert_runs(w1[0], w3[0], w2[0])
    n_grp = w_rg.shape[2]
    wr = jnp.zeros((d, LANES), F32).at[:, :n_exp].set(w_re[0]).at[:, n_exp:n_exp + n_grp].set(w_rg[0])
    br = jnp.zeros((1, LANES), F32).at[0, :n_exp].set(b_re[0]).at[0, n_exp:n_exp + n_grp].set(b_rg[0])

    mk_p, mv_p = _memkv(mem_prompt, g_mem_in[0], wmkv_bf)
    pos_p = jnp.arange(s, dtype=jnp.int32)
    u_p, qt_p, kt_p, vt_p, kb_p, vtb_p = _inproj(x_prompt, pos_p, g_mix[0], wtok_bf, wfeat_bf,
                                                 min(s, 512), True)
    oa_p = _moba_prompt(qt_p, kb_p, vtb_p, vt_p)
    x2_p = _mix_prompt(x_prompt, u_p, oa_p, wp_bf, s_pool[0], wout_bf, g_mem[0], wq_bf,
                       mk_p, mv_p, wo_bf, min(s, 512))
    y_p = _moe(x2_p.reshape(b * s, d), g_ffn[0], wr, br, w1_bf, w3_bf, w2_bf, g_final,
               min(b * s, 1024)).reshape(b, s, d)

    pos_s = jnp.full((bd,), past_len, dtype=jnp.int32)
    u_s, qt_s, kt_s, vt_s = _inproj(x_sample.reshape(1, bd, d), pos_s, g_mix[0], wtok_bf, wfeat_bf,
                                    bd, False)
    u_s = u_s.reshape(bd, width)
    q_s, k_s, v_s = qt_s[0].T, kt_s[0].T, vt_s[0].T
    ckt = jnp.transpose(cache_k[0], (0, 2, 3, 1)).reshape(cache_k.shape[1], width, PAGE_SIZE)
    cvt = jnp.transpose(cache_v[0], (0, 2, 3, 1)).reshape(cache_v.shape[1], width, PAGE_SIZE)
    top = _page_gate(ckt, page_table, q_s, k_s)
    oa_s = _moba_sample(ckt, cvt, page_table, top, q_s, k_s, v_s)
    state_t = jnp.transpose(state_pool[0], (1, 0, 2))
    x1_s = _mix_sample(x_sample.reshape(bd, d), u_s, state_t, oa_s, wp_bf, s_pool[0], wout_bf)
    x2_s = _mem_sample(x1_s, g_mem[0], wq_bf, cache_mem_k[0], cache_mem_v[0], wo_bf)
    y_s = _moe(x2_s, g_ffn[0], wr, br, w1_bf, w3_bf, w2_bf, g_final, bd).reshape(bd, 1, d)

    def heads(t_feat, n):
        lead = t_feat.shape[0]
        return jnp.transpose(t_feat.reshape(lead, n_heads, HEAD_DIM, n), (0, 3, 1, 2))[None]

    mem_heads = lambda t: t.reshape(1, b, t.shape[1], -1, MEM_HEAD_DIM)
    new_pool_p = u_p[:, s - POOL_BUF:][None]
    new_pool_s = jnp.concatenate([state_pool[0][:, 1:], u_s[:, None, :]], axis=1)[None]
    new_k_s = jnp.transpose(heads(kt_s, bd), (0, 2, 1, 3, 4))
    new_v_s = jnp.transpose(heads(vt_s, bd), (0, 2, 1, 3, 4))
    return (y_p, y_s, heads(kt_p, s), heads(vt_p, s), new_pool_p, mem_heads(mk_p),
            mem_heads(mv_p), new_k_s, new_v_s, new_pool_s)
```

```python
import functools

import jax
import jax.numpy as jnp
from jax import lax
from jax.experimental import pallas as pl
from jax.experimental.pallas import tpu as pltpu

F32 = jnp.float32
BF16 = jnp.bfloat16

POOL_WINDOWS = (2, 4, 8, 16)
POOL_BUF = max(POOL_WINDOWS) - 1
HEAD_DIM = 64
ROPE_DIM = HEAD_DIM // 4
ROPE_THETA = 500000.0
MOBA_BLOCK = 256
MOBA_TOPK = 3
MOBA_GROUP = 4
PAGE_SIZE = 128
MEM_HEAD_DIM = 128
N_GROUPS = 4
EXPERTS_PER_GROUP = 8
MOE_EXPERTS_PER_STEP = 4
RMS_EPS = 1e-6
NEG_INF = -1e30
LOG2_E = 1.4426950408889634

LANES = 128
VMEM_LIMIT = 56 * 1024 * 1024


def _params(*sem):
    return pltpu.CompilerParams(dimension_semantics=sem, vmem_limit_bytes=VMEM_LIMIT)


def _rms(x, g):
    ms = jnp.mean(x * x, axis=-1, keepdims=True)
    return x * lax.rsqrt(ms + RMS_EPS) * g


def _dot(a, b):
    return jnp.dot(a, b, preferred_element_type=F32)


def _dot_nt(a, b):
    return lax.dot_general(a, b, (((1,), (1,)), ((), ())), preferred_element_type=F32)


def _dot_exact(a, b):
    return jnp.dot(a, b, preferred_element_type=F32, precision=lax.Precision.HIGHEST)


def _idiv(x, n):
    shift = n.bit_length() - 1
    assert 1 << shift == n
    return jnp.right_shift(x, shift)


def _const_spec(shape):
    return pl.BlockSpec(shape, lambda *_: (0,) * len(shape))


def _diag_mask(n):
    return (lax.broadcasted_iota(jnp.int32, (n, n), 0) == lax.broadcasted_iota(jnp.int32, (n, n), 1))


def _inproj_kernel(x_ref, g_ref, wtok_ref, wfeat_ref, cos_ref, sa_ref, sb_ref, cos_t_ref,
                   sa_t_ref, sb_t_ref, u_ref, qt_ref, kt_ref, vt_ref, *attn_refs):
    width = u_ref.shape[-1]
    half = ROPE_DIM // 2
    n_pairs = width // LANES
    h = _rms(x_ref[...], g_ref[...]).astype(BF16)

    def rope_feat(t):
        outs = []
        for c in range(n_pairs):
            tc = t[c * LANES:(c + 1) * LANES, :]
            outs.append(tc * cos_t_ref[...] + pltpu.roll(tc, LANES - half, axis=0) * sa_t_ref[...]
                        + pltpu.roll(tc, half, axis=0) * sb_t_ref[...])
        return jnp.concatenate(outs, axis=0)

    def rope_tok(t):
        outs = []
        for c in range(n_pairs):
            tc = t[:, c * LANES:(c + 1) * LANES]
            outs.append(tc * cos_ref[...] + pltpu.roll(tc, LANES - half, axis=1) * sa_ref[...]
                        + pltpu.roll(tc, half, axis=1) * sb_ref[...])
        return jnp.concatenate(outs, axis=1)

    feat = _dot_nt(wfeat_ref[...], h)
    qt_ref[...] = rope_feat(feat[:width]) * (HEAD_DIM ** -0.5)
    kt_ref[...] = rope_feat(feat[width:2 * width])
    vt = feat[2 * width:]
    vt_ref[...] = vt
    if attn_refs:
        kb_ref, vtb_ref = attn_refs
        tok = _dot(h, wtok_ref[...])
        u_ref[...] = tok[:, :width]
        kb_ref[...] = rope_tok(tok[:, width:]).astype(BF16)
        for jj in range(vtb_ref.shape[0]):
            vtb_ref[jj] = vt[:, jj * MOBA_BLOCK:(jj + 1) * MOBA_BLOCK].astype(BF16)
    else:
        u_ref[...] = _dot(h, wtok_ref[:, :width])


def _rope_tables(pos):
    half = ROPE_DIM // 2
    inv_freq = ROPE_THETA ** (-jnp.arange(half, dtype=F32) / half)
    ang = pos.astype(F32)[:, None] * inv_freq[None, :]
    cos, sin = jnp.cos(ang), jnp.sin(ang)
    n = pos.shape[0]
    pad = jnp.zeros((n, HEAD_DIM - ROPE_DIM), F32)
    cos_h = jnp.concatenate([cos, cos, pad + 1.0], axis=1)
    sa_h = jnp.concatenate([-sin, jnp.zeros_like(sin), pad], axis=1)
    sb_h = jnp.concatenate([jnp.zeros_like(sin), sin, pad], axis=1)
    rep = LANES // HEAD_DIM
    return tuple(jnp.tile(t, (1, rep)) for t in (cos_h, sa_h, sb_h))


def _inproj(x, pos, g_mix, wtok_bf, wfeat_bf, tm, attn_copies):
    b, s, d = x.shape
    width = wtok_bf.shape[1] // 2
    tabs = _rope_tables(pos)
    tabs_t = tuple(t.T for t in tabs)
    tok = lambda last: pl.BlockSpec((None, tm, last), lambda bi, i: (bi, i, 0))
    feat = pl.BlockSpec((None, width, tm), lambda bi, i: (bi, 0, i))
    tab = pl.BlockSpec((tm, LANES), lambda bi, i: (i, 0))
    tab_t = pl.BlockSpec((LANES, tm), lambda bi, i: (0, i))
    feat_shape = jax.ShapeDtypeStruct((b, width, s), F32)
    out_specs = [tok(width), feat, feat, feat]
    out_shape = [jax.ShapeDtypeStruct((b, s, width), F32), feat_shape, feat_shape, feat_shape]
    if attn_copies:
        per = tm // MOBA_BLOCK
        out_specs += [tok(width),
                      pl.BlockSpec((None, per, width, MOBA_BLOCK), lambda bi, i: (bi, i, 0, 0))]
        out_shape += [jax.ShapeDtypeStruct((b, s, width), BF16),
                      jax.ShapeDtypeStruct((b, s // MOBA_BLOCK, width, MOBA_BLOCK), BF16)]
    return pl.pallas_call(
        _inproj_kernel,
        grid=(b, s // tm),
        in_specs=[tok(d), _const_spec((1, d)), _const_spec(wtok_bf.shape),
                  _const_spec(wfeat_bf.shape), tab, tab, tab, tab_t, tab_t, tab_t],
        out_specs=out_specs,
        out_shape=out_shape,
        compiler_params=_params("parallel", "parallel"),
        name="inproj",
    )(x, g_mix.reshape(1, d), wtok_bf, wfeat_bf, *tabs, *tabs_t)


def _moba_prompt_kernel(qt_ref, k_ref, vt_ref, vo_ref, o_ref, kmean_sc):
    i = pl.program_id(2)
    nb = vt_ref.shape[0]
    blk = qt_ref.shape[1]

    @pl.when(i == 0)
    def _():
        r = lax.broadcasted_iota(jnp.int32, (nb, nb * blk), 1)
        j = lax.broadcasted_iota(jnp.int32, (nb, nb * blk), 0)
        onehot = jnp.where(_idiv(r, blk) == j, 1.0, 0.0).astype(BF16)
        kmean_sc[...] = _dot(onehot, k_ref[...]) / blk

    qt = qt_ref[...] * LOG2_E
    feat = lax.broadcasted_iota(jnp.int32, (LANES, 1), 0)
    qbd = jnp.concatenate([jnp.where(_idiv(feat, HEAD_DIM) == hd, qt, 0.0) for hd in range(2)],
                          axis=1)
    qb = qbd.astype(BF16)
    blk_id = lax.broadcasted_iota(jnp.int32, (nb, 2 * blk), 0)
    gate = jnp.where(blk_id < i, _dot_exact(kmean_sc[...], qbd), -jnp.inf)
    sel = jnp.zeros((nb, 2 * blk), F32)
    for _ in range(MOBA_TOPK):
        gmax = jnp.max(gate, axis=0, keepdims=True)
        first = jnp.min(jnp.where(gate == gmax, blk_id, nb), axis=0, keepdims=True)
        pick = blk_id == first
        sel = jnp.where(pick, 1.0, sel)
        gate = jnp.where(pick, -jnp.inf, gate)
    sel = jnp.where(blk_id < i, sel, 0.0)

    def attend(st, vts, carry):
        m, l, acc = carry
        m_new = jnp.maximum(m, jnp.max(st, axis=0, keepdims=True))
        alpha = jnp.exp2(m - m_new)
        p = jnp.exp2(st - m_new)
        l_new = alpha * l + jnp.sum(p, axis=0, keepdims=True)
        pb = p.astype(BF16)
        acc = alpha * acc
        for g, vt in enumerate(vts):
            acc = acc + _dot(vt, pb[g * blk:(g + 1) * blk])
        return m_new, l_new, acc

    carry = (jnp.full((1, 2 * blk), NEG_INF, F32), jnp.zeros((1, 2 * blk), F32),
             jnp.zeros((LANES, 2 * blk), F32))
    key_id = lax.broadcasted_iota(jnp.int32, (blk, 2 * blk), 0)
    qry_id = lax.broadcasted_iota(jnp.int32, (blk, 2 * blk), 1) & (blk - 1)
    own0 = pl.multiple_of(i * blk, blk)
    s_own = jnp.where(key_id <= qry_id, _dot(k_ref[pl.ds(own0, blk), :], qb), NEG_INF)
    carry = attend(s_own, [vo_ref[...].astype(BF16)], carry)

    def body(t, carry):
        r0 = pl.multiple_of(t * (MOBA_GROUP * blk), MOBA_GROUP * blk)
        s = _dot(k_ref[pl.ds(r0, MOBA_GROUP * blk), :], qb)
        segs, vts = [], []
        for g in range(MOBA_GROUP):
            jb = t * MOBA_GROUP + g
            chosen = jnp.sum(jnp.where(blk_id == jb, sel, 0.0), axis=0, keepdims=True) > 0.5
            segs.append(jnp.where(chosen, s[g * blk:(g + 1) * blk], NEG_INF))
            vts.append(vt_ref[jb])
        return attend(jnp.concatenate(segs, axis=0), vts, carry)

    n_groups = lax.shift_right_logical(i + (MOBA_GROUP - 1), MOBA_GROUP.bit_length() - 1)
    m, l, acc = lax.fori_loop(0, n_groups, body, carry)
    ot = jnp.concatenate([acc[hd * HEAD_DIM:(hd + 1) * HEAD_DIM, hd * blk:(hd + 1) * blk]
                          / l[:, hd * blk:(hd + 1) * blk] for hd in range(2)], axis=0)
    o_ref[...] = ot.T.astype(o_ref.dtype)


def _moba_prompt(qt, kb, vtb, vt):
    b, width, s = qt.shape
    nb = s // MOBA_BLOCK
    assert nb % MOBA_GROUP == 0
    blk = MOBA_BLOCK
    qspec = pl.BlockSpec((None, LANES, blk), lambda bi, hp, i: (bi, hp, i))
    kspec = pl.BlockSpec((None, s, LANES), lambda bi, hp, i: (bi, 0, hp))
    vspec = pl.BlockSpec((None, nb, LANES, blk), lambda bi, hp, i: (bi, 0, hp, 0))
    ospec = pl.BlockSpec((None, blk, LANES), lambda bi, hp, i: (bi, i, hp))
    return pl.pallas_call(
        _moba_prompt_kernel,
        grid=(b, width // LANES, nb),
        in_specs=[qspec, kspec, vspec, qspec],
        out_specs=ospec,
        out_shape=jax.ShapeDtypeStruct((b, s, width), BF16),
        scratch_shapes=[pltpu.VMEM((nb, LANES), F32)],
        compiler_params=_params("parallel", "parallel", "arbitrary"),
        name="moba_prompt",
    )(qt, kb, vtb, vt)


def _memkv_kernel(mem_ref, g_ref, w_ref, mk_ref, mv_ref):
    width = mk_ref.shape[-1]
    hm = _rms(mem_ref[...], g_ref[...]).astype(BF16)
    kv = _dot(hm, w_ref[...])
    mk_ref[...] = kv[:, :width]
    mv_ref[...] = kv[:, width:]


def _memkv(mem, g_mem_in, w_mkv_bf):
    b, m, d = mem.shape
    width = w_mkv_bf.shape[1] // 2
    out = jax.ShapeDtypeStruct((b, m, width), F32)
    return pl.pallas_call(
        _memkv_kernel,
        grid=(b,),
        in_specs=[pl.BlockSpec((None, m, d), lambda bi: (bi, 0, 0)), _const_spec((1, d)),
                  _const_spec(w_mkv_bf.shape)],
        out_specs=[pl.BlockSpec((None, m, width), lambda bi: (bi, 0, 0))] * 2,
        out_shape=[out, out],
        compiler_params=_params("parallel"),
        name="memkv",
    )(mem, g_mem_in.reshape(1, d), w_mkv_bf)


def _pool_mix(z_in_groups, wp_ref, sp_ref):
    zs = [_dot(d.astype(BF16), wp_ref[g]) for g, d in enumerate(z_in_groups)]
    return jnp.concatenate(zs, axis=1) * sp_ref[...]


def _cross_attend(x1, gm_ref, wq_ref, mk, mv, wo_ref):
    hm = _rms(x1, gm_ref[...]).astype(BF16)
    qm = _dot(hm, wq_ref[...])
    outs = []
    for hh in range(qm.shape[1] // MEM_HEAD_DIM):
        sl = slice(hh * MEM_HEAD_DIM, (hh + 1) * MEM_HEAD_DIM)
        s = _dot_nt(qm[:, sl].astype(BF16), mk[:, sl]) * (MEM_HEAD_DIM ** -0.5)
        p = jnp.exp(s - jnp.max(s, axis=1, keepdims=True))
        p = p / jnp.sum(p, axis=1, keepdims=True)
        outs.append(_dot(p.astype(BF16), mv[:, sl]))
    o = jnp.concatenate(outs, axis=1).astype(BF16)
    return x1 + _dot(o, wo_ref[...])


def _mix_prompt_kernel(x_ref, u_ref, halo_ref, oa_ref, wp_ref, sp_ref, wout_ref, gm_ref, wq_ref,
                       mk_ref, mv_ref, wo_ref, x2_ref):
    i = pl.program_id(1)
    tm, width = u_ref.shape
    gw = width // len(POOL_WINDOWS)
    halo_rows = halo_ref.shape[0]
    u = u_ref[...]
    halo = jnp.where(i > 0, halo_ref[...], 0.0)
    ext = jnp.concatenate([halo, u], axis=0)
    pos = i * tm + lax.broadcasted_iota(jnp.int32, (tm, 1), 0)
    groups = []
    for g, w in enumerate(POOL_WINDOWS):
        a = ext[:, g * gw:(g + 1) * gw]
        span = 1
        while span < w:
            a = a + pltpu.roll(a, span, axis=0)
            span *= 2
        cnt = jnp.minimum(w, pos + 1).astype(F32)
        groups.append(a[halo_rows:] / cnt - u[:, g * gw:(g + 1) * gw])
    z = _pool_mix(groups, wp_ref, sp_ref).astype(BF16)
    x1 = x_ref[...] + _dot(z, wout_ref[:width, :]) + _dot(oa_ref[...], wout_ref[width:, :])
    x2_ref[...] = _cross_attend(x1, gm_ref, wq_ref, mk_ref[...].astype(BF16),
                                mv_ref[...].astype(BF16), wo_ref)


def _mix_prompt(x, u, o_attn, wp_bf, s_pool, wout_bf, g_mem, wq_bf, mk, mv, wo_bf, tm):
    b, s, d = x.shape
    width = u.shape[-1]
    m = mk.shape[1]
    halo = 16
    tok = lambda last: pl.BlockSpec((None, tm, last), lambda bi, i: (bi, i, 0))
    halo_spec = pl.BlockSpec((None, halo, width),
                             lambda bi, i: (bi, jnp.maximum(i * (tm // halo) - 1, 0), 0))
    mem_spec = pl.BlockSpec((None, m, mk.shape[-1]), lambda bi, i: (bi, 0, 0))
    return pl.pallas_call(
        _mix_prompt_kernel,
        grid=(b, s // tm),
        in_specs=[tok(d), tok(width), halo_spec, tok(width), _const_spec(wp_bf.shape),
                  _const_spec((1, width)), _const_spec(wout_bf.shape), _const_spec((1, d)),
                  _const_spec(wq_bf.shape), mem_spec, mem_spec, _const_spec(wo_bf.shape)],
        out_specs=tok(d),
        out_shape=jax.ShapeDtypeStruct((b, s, d), F32),
        compiler_params=_params("parallel", "parallel"),
        name="mix_prompt",
    )(x, u, u, o_attn, wp_bf, s_pool.reshape(1, width), wout_bf, g_mem.reshape(1, d), wq_bf,
      mk, mv, wo_bf)


def _moe_kernel(x_ref, gf_ref, wr_ref, br_ref, w1_ref, w3_ref, w2_ref, gfin_ref, y_ref,
                h_sc, comb_sc, acc_sc):
    e = pl.program_id(1)
    n_exp = pl.num_programs(1)
    tm = x_ref.shape[0]
    lane = lax.broadcasted_iota(jnp.int32, (tm, LANES), 1)

    @pl.when(e == 0)
    def _():
        t = _rms(x_ref[...], gf_ref[...])
        h_sc[...] = t.astype(BF16)
        logits = _dot_exact(t, wr_ref[...]) + br_ref[...]
        n_e = N_GROUPS * EXPERTS_PER_GROUP
        is_g = (lane >= n_e) & (lane < n_e + N_GROUPS)
        lg = jnp.where(is_g, logits, -jnp.inf)
        gmax = jnp.max(lg, axis=1, keepdims=True)
        grp = jnp.min(jnp.where(lg == gmax, lane, LANES), axis=1, keepdims=True) - n_e
        p_grp = 1.0 / jnp.sum(jnp.where(is_g, jnp.exp(logits - gmax), 0.0), axis=1, keepdims=True)
        le = jnp.where(_idiv(lane, EXPERTS_PER_GROUP) == grp, logits, -jnp.inf)
        t1 = jnp.max(le, axis=1, keepdims=True)
        i1 = jnp.min(jnp.where(le == t1, lane, LANES), axis=1, keepdims=True)
        le2 = jnp.where(lane == i1, -jnp.inf, le)
        t2 = jnp.max(le2, axis=1, keepdims=True)
        i2 = jnp.min(jnp.where(le2 == t2, lane, LANES), axis=1, keepdims=True)
        r = jnp.exp(t2 - t1)
        g1 = 1.0 / (1.0 + r)
        g2 = r / (1.0 + r)
        comb_sc[...] = p_grp * (jnp.where(lane == i1, g1, 0.0) + jnp.where(lane == i2, g2, 0.0))
        acc_sc[...] = jnp.zeros_like(acc_sc)

    h = h_sc[...]
    comb = comb_sc[...]
    run, hid, d = w2_ref.shape
    scaled = []
    for k in range(run):
        a1 = _dot(h, w1_ref[k])
        a3 = _dot(h, w3_ref[k])
        c = jnp.sum(jnp.where(lane == e * run + k, comb, 0.0), axis=1, keepdims=True)
        scaled.append(((a1 * (1.0 / (1.0 + jnp.exp(-a1)))) * a3 * c).astype(BF16))
    acc_sc[...] += _dot(jnp.concatenate(scaled, axis=1), w2_ref[...].reshape(run * hid, d))

    @pl.when(e == n_exp - 1)
    def _():
        y_ref[...] = _rms(x_ref[...] + acc_sc[...], gfin_ref[...])


def _moe(x, g_ffn, wr, br, w1_bf, w3_bf, w2_bf, g_final, tm):
    t, d = x.shape
    n_exp, _, hid = w1_bf.shape
    run = MOE_EXPERTS_PER_STEP
    assert n_exp % run == 0
    tok = pl.BlockSpec((tm, d), lambda i, e: (i, 0))
    return pl.pallas_call(
        _moe_kernel,
        grid=(t // tm, n_exp // run),
        in_specs=[tok, _const_spec((1, d)), _const_spec(wr.shape), _const_spec((1, LANES)),
                  pl.BlockSpec((run, d, hid), lambda i, e: (e, 0, 0)),
                  pl.BlockSpec((run, d, hid), lambda i, e: (e, 0, 0)),
                  pl.BlockSpec((run, hid, d), lambda i, e: (e, 0, 0)),
                  _const_spec((1, d))],
        out_specs=tok,
        out_shape=jax.ShapeDtypeStruct((t, d), F32),
        scratch_shapes=[pltpu.VMEM((tm, d), BF16), pltpu.VMEM((tm, LANES), F32),
                        pltpu.VMEM((tm, d), F32)],
        compiler_params=_params("parallel", "arbitrary"),
        name="moe",
    )(x, g_ffn.reshape(1, d), wr, br, w1_bf, w3_bf, w2_bf, g_final.reshape(1, d))


def _page_gate_kernel(pt_ref, *refs, pages_per_step, n_blocks):
    page_refs = refs[:pages_per_step]
    q_ref, kn_ref, top_ref, ksum_sc = refs[pages_per_step:]
    c = pl.program_id(1)
    width = q_ref.shape[-1]
    n_heads = width // HEAD_DIM
    per_blk = MOBA_BLOCK // PAGE_SIZE
    n_cols = pages_per_step // per_blk
    lane = lax.broadcasted_iota(jnp.int32, (1, LANES), 1)

    @pl.when(c == 0)
    def _():
        ksum_sc[...] = jnp.zeros_like(ksum_sc)

    upd = ksum_sc[...]
    for r in range(n_cols):
        pages = [page_refs[r * per_blk + g][...] for g in range(per_blk)]
        col = jnp.sum(functools.reduce(lambda a, b: a + b, pages), axis=1, keepdims=True)
        upd = jnp.where(lane == c * n_cols + r, col, upd)
    ksum_sc[...] = upd

    @pl.when(c == pl.num_programs(1) - 1)
    def _():
        hrow = lax.broadcasted_iota(jnp.int32, (n_heads, width), 0)
        feat = lax.broadcasted_iota(jnp.int32, (n_heads, width), 1)
        qblk = jnp.where(_idiv(feat, HEAD_DIM) == hrow, q_ref[...], 0.0)
        blk = lax.broadcasted_iota(jnp.int32, (n_heads, LANES), 1)
        gate_past = _dot_exact(qblk, ksum_sc[...] / MOBA_BLOCK)
        gate_own = jnp.sum(qblk * (kn_ref[...] / MOBA_BLOCK), axis=1, keepdims=True)
        gate = jnp.where(blk == n_blocks, gate_own, gate_past)
        gate = jnp.where(blk < n_blocks, gate, -jnp.inf)
        slot = lax.broadcasted_iota(jnp.int32, (n_heads, MOBA_TOPK), 1)
        top = jnp.zeros((n_heads, MOBA_TOPK), jnp.int32)
        for t in range(MOBA_TOPK):
            gmax = jnp.max(gate, axis=1, keepdims=True)
            first = jnp.min(jnp.where(gate == gmax, blk, LANES), axis=1, keepdims=True)
            top = jnp.where(slot == t, first, top)
            gate = jnp.where(blk == first, -jnp.inf, gate)
        top_ref[...] = top


def _page_gate(cache_kt, page_table, q_s, k_new):
    bd, n_pages = page_table.shape
    width = q_s.shape[-1]
    n_blocks = n_pages * PAGE_SIZE // MOBA_BLOCK
    assert n_blocks < LANES
    pages_per_step = min(32, n_pages)
    assert n_pages % pages_per_step == 0
    steps = n_pages // pages_per_step
    n_heads = width // HEAD_DIM

    def page_spec(j):
        return pl.BlockSpec(
            (None, width, PAGE_SIZE),
            lambda b, c, pt: (pt[b * n_pages + c * pages_per_step + j], 0, 0))

    row = pl.BlockSpec((None, 1, width), lambda b, c, pt: (b, 0, 0))
    grid_spec = pltpu.PrefetchScalarGridSpec(
        num_scalar_prefetch=1,
        grid=(bd, steps),
        in_specs=[page_spec(j) for j in range(pages_per_step)] + [row, row],
        out_specs=pl.BlockSpec((None, n_heads, MOBA_TOPK), lambda b, c, pt: (b, 0, 0)),
        scratch_shapes=[pltpu.VMEM((width, LANES), F32)],
    )
    kern = functools.partial(_page_gate_kernel, pages_per_step=pages_per_step, n_blocks=n_blocks)
    return pl.pallas_call(
        kern,
        grid_spec=grid_spec,
        out_shape=jax.ShapeDtypeStruct((bd, n_heads, MOBA_TOPK), jnp.int32),
        compiler_params=_params("parallel", "arbitrary"),
        name="page_gate",
    )(page_table.reshape(-1), *([cache_kt] * pages_per_step), q_s.reshape(bd, 1, width),
      k_new.reshape(bd, 1, width))


def _moba_sample_kernel(pt_ref, top_ref, ck_hbm, cv_hbm, q_ref, kn_ref, vn_ref, o_ref,
                        kbuf, vbuf, sems, *, n_pages):
    b = pl.program_id(0)
    n_samples = pl.num_programs(0)
    width = q_ref.shape[-1]
    n_heads = width // HEAD_DIM
    per_blk = MOBA_BLOCK // PAGE_SIZE
    n_sel_pages = MOBA_TOPK * per_blk

    def tile_copies(page, head, slot, idx):
        rows = pl.ds(head * HEAD_DIM, HEAD_DIM)
        return (pltpu.make_async_copy(ck_hbm.at[page, rows, :], kbuf.at[slot, idx], sems.at[slot, 0]),
                pltpu.make_async_copy(cv_hbm.at[page, rows, :], vbuf.at[slot, idx], sems.at[slot, 1]))

    def start_sample(sample, slot):
        for head in range(n_heads):
            for t in range(MOBA_TOPK):
                blk = top_ref[(sample * n_heads + head) * MOBA_TOPK + t]
                for pg in range(per_blk):
                    page = pt_ref[sample * n_pages + blk * per_blk + pg]
                    for cp in tile_copies(page, head, slot, (head * MOBA_TOPK + t) * per_blk + pg):
                        cp.start()

    def wait_sample(slot):
        for idx in range(n_heads * n_sel_pages):
            for cp in tile_copies(0, 0, slot, idx):
                cp.wait()

    slot = lax.rem(b, 2)

    @pl.when(b == 0)
    def _():
        start_sample(0, 0)

    @pl.when(b + 1 < n_samples)
    def _():
        start_sample(b + 1, 1 - slot)

    wait_sample(slot)

    lane = lax.broadcasted_iota(jnp.int32, (1, LANES), 1)
    diag = _diag_mask(LANES)
    first = lane < HEAD_DIM
    outs = []
    for hp in range(width // LANES):
        sl = slice(hp * LANES, (hp + 1) * LANES)
        q = q_ref[:, sl]
        kn = kn_ref[:, sl]
        qcol = jnp.sum(jnp.where(diag, q, 0.0), axis=1, keepdims=True)
        ocols, p_owns, dens = [], [], []
        for hd in range(2):
            base = (2 * hp + hd) * n_sel_pages
            qc = qcol[hd * HEAD_DIM:(hd + 1) * HEAD_DIM]
            in_head = _idiv(lane, HEAD_DIM) == hd
            s_own = jnp.sum(jnp.where(in_head, q * kn, 0.0), axis=1, keepdims=True)
            scores = [jnp.sum(kbuf[slot, base + r] * qc, axis=0, keepdims=True)
                      for r in range(n_sel_pages)]
            m = s_own
            for s in scores:
                m = jnp.maximum(m, jnp.max(s, axis=1, keepdims=True))
            p_own = jnp.exp(s_own - m)
            den = p_own
            acc = jnp.zeros((HEAD_DIM, PAGE_SIZE), F32)
            for r, s in enumerate(scores):
                p = jnp.exp(s - m)
                den = den + jnp.sum(p, axis=1, keepdims=True)
                acc = acc + vbuf[slot, base + r] * p
            ocols.append(jnp.sum(acc, axis=1, keepdims=True))
            p_owns.append(p_own)
            dens.append(den)
        ocol = jnp.concatenate(ocols, axis=0)
        orow = jnp.sum(jnp.where(diag, ocol, 0.0), axis=0, keepdims=True)
        p_own = jnp.where(first, p_owns[0], p_owns[1])
        den = jnp.where(first, dens[0], dens[1])
        outs.append((orow + p_own * vn_ref[:, sl]) / den)
    o_ref[...] = jnp.concatenate(outs, axis=1)


def _moba_sample(cache_kt, cache_vt, page_table, top, q_s, k_new, v_new):
    bd, n_pages = page_table.shape
    width = q_s.shape[-1]
    n_heads = width // HEAD_DIM
    n_tiles = n_heads * MOBA_TOPK * (MOBA_BLOCK // PAGE_SIZE)
    row = pl.BlockSpec((None, 1, width), lambda b, pt, tp: (b, 0, 0))
    hbm = pl.BlockSpec(memory_space=pl.ANY)
    grid_spec = pltpu.PrefetchScalarGridSpec(
        num_scalar_prefetch=2,
        grid=(bd,),
        in_specs=[hbm, hbm, row, row, row],
        out_specs=row,
        scratch_shapes=[pltpu.VMEM((2, n_tiles, HEAD_DIM, PAGE_SIZE), F32),
                        pltpu.VMEM((2, n_tiles, HEAD_DIM, PAGE_SIZE), F32),
                        pltpu.SemaphoreType.DMA((2, 2))],
    )
    kern = functools.partial(_moba_sample_kernel, n_pages=n_pages)
    as_rows = lambda t: t.reshape(bd, 1, width)
    out = pl.pallas_call(
        kern,
        grid_spec=grid_spec,
        out_shape=jax.ShapeDtypeStruct((bd, 1, width), F32),
        compiler_params=_params("arbitrary"),
        name="moba_sample",
    )(page_table.reshape(-1), top.reshape(-1), cache_kt, cache_vt,
      as_rows(q_s), as_rows(k_new), as_rows(v_new))
    return out.reshape(bd, width)


def _mix_sample_kernel(x_ref, u_ref, st_ref, oa_ref, wp_ref, sp_ref, wout_ref, x1_ref):
    width = u_ref.shape[-1]
    gw = width // len(POOL_WINDOWS)
    u = u_ref[...]
    groups = []
    for g, w in enumerate(POOL_WINDOWS):
        sl = slice(g * gw, (g + 1) * gw)
        a = u[:, sl]
        for back in range(1, w):
            a = a + st_ref[POOL_BUF - back][:, sl]
        groups.append(a / float(w) - u[:, sl])
    z = _pool_mix(groups, wp_ref, sp_ref).astype(BF16)
    x1_ref[...] = (x_ref[...] + _dot(z, wout_ref[:width, :])
                   + _dot(oa_ref[...].astype(BF16), wout_ref[width:, :]))


def _mix_sample(x, u, state_t, o_attn, wp_bf, s_pool, wout_bf):
    t, d = x.shape
    width = u.shape[-1]
    return pl.pallas_call(
        _mix_sample_kernel,
        out_shape=jax.ShapeDtypeStruct((t, d), F32),
        compiler_params=pltpu.CompilerParams(vmem_limit_bytes=VMEM_LIMIT),
        name="mix_sample",
    )(x, u, state_t, o_attn, wp_bf, s_pool.reshape(1, width), wout_bf)


def _mem_sample_kernel(x1_ref, gm_ref, wq_ref, mk_ref, mv_ref, wo_ref, x2_ref):
    x1 = x1_ref[...]
    n = x1.shape[0]
    n_mem_heads = mk_ref.shape[2]
    hm = _rms(x1, gm_ref[...]).astype(BF16)
    qm = _dot(hm, wq_ref[...])
    rows = []
    for r in range(n):
        outs = []
        for hh in range(n_mem_heads):
            sl = slice(hh * MEM_HEAD_DIM, (hh + 1) * MEM_HEAD_DIM)
            kh = mk_ref[r, :, hh, :]
            vh = mv_ref[r, :, hh, :]
            s = jnp.sum(kh * qm[r:r + 1, sl], axis=1, keepdims=True) * (MEM_HEAD_DIM ** -0.5)
            p = jnp.exp(s - jnp.max(s, axis=0, keepdims=True))
            p = p / jnp.sum(p, axis=0, keepdims=True)
            outs.append(jnp.sum(p * vh, axis=0, keepdims=True))
        rows.append(jnp.concatenate(outs, axis=1))
    o = jnp.concatenate(rows, axis=0).astype(BF16)
    x2_ref[...] = x1 + _dot(o, wo_ref[...])


def _mem_sample(x1, g_mem, wq_bf, mem_k, mem_v, wo_bf):
    t, d = x1.shape
    m, mh, md = mem_k.shape[1:]
    n = 8
    tok = pl.BlockSpec((n, d), lambda i: (i, 0))
    mem = pl.BlockSpec((n, m, mh, md), lambda i: (i, 0, 0, 0))
    return pl.pallas_call(
        _mem_sample_kernel,
        grid=(t // n,),
        in_specs=[tok, _const_spec((1, d)), _const_spec(wq_bf.shape), mem, mem,
                  _const_spec(wo_bf.shape)],
        out_specs=tok,
        out_shape=jax.ShapeDtypeStruct((t, d), F32),
        compiler_params=_params("parallel"),
        name="mem_sample",
    )(x1, g_mem.reshape(1, d), wq_bf, mem_k, mem_v, wo_bf)


def kernel(x_prompt, x_sample, mem_prompt, cache_k, cache_v, cache_mem_k, cache_mem_v, state_pool,
           page_table, g_mix, w_in, w_pool, s_pool, w_out, g_mem, g_mem_in, w_mq, w_mk, w_mv, w_mo,
           g_ffn, w_rg, b_rg, w_re, b_re, w1, w3, w2, g_final):
    assert g_mix.shape[0] == 1, "one layer"
    b, s, d = x_prompt.shape
    bd, n_new, _ = x_sample.shape
    assert n_new == 1 and s % MOBA_BLOCK == 0
    n_pages = page_table.shape[1]
    past_len = n_pages * PAGE_SIZE
    assert past_len % MOBA_BLOCK == 0 and past_len // MOBA_BLOCK >= MOBA_TOPK
    width = w_in.shape[2] // 4
    n_exp = w1.shape[1]
    n_heads = width // HEAD_DIM

    w_in0 = w_in[0]
    wtok_bf = jnp.concatenate([w_in0[:, :width], w_in0[:, 2 * width:3 * width]], axis=1).astype(BF16)
    wfeat_bf = w_in0[:, width:].T.astype(BF16)
    wp_bf = w_pool[0].astype(BF16)
    wout_bf = w_out[0].astype(BF16)
    wq_bf = w_mq[0].astype(BF16)
    wmkv_bf = jnp.concatenate([w_mk[0], w_mv[0]], axis=1).astype(BF16)
    wo_bf = w_mo[0].astype(BF16)
    w1_bf, w3_bf, w2_bf = w1[0].astype(BF16), w3[0].astype(BF16), w2[0].astype(BF16)
    n_grp = w_rg.shape[2]
    wr = jnp.zeros((d, LANES), F32).at[:, :n_exp].set(w_re[0]).at[:, n_exp:n_exp + n_grp].set(w_rg[0])
    br = jnp.zeros((1, LANES), F32).at[0, :n_exp].set(b_re[0]).at[0, n_exp:n_exp + n_grp].set(b_rg[0])

    mk_p, mv_p = _memkv(mem_prompt, g_mem_in[0], wmkv_bf)
    pos_p = jnp.arange(s, dtype=jnp.int32)
    u_p, qt_p, kt_p, vt_p, kb_p, vtb_p = _inproj(x_prompt, pos_p, g_mix[0], wtok_bf, wfeat_bf,
                                                 min(s, 512), True)
    oa_p = _moba_prompt(qt_p, kb_p, vtb_p, vt_p)
    x2_p = _mix_prompt(x_prompt, u_p, oa_p, wp_bf, s_pool[0], wout_bf, g_mem[0], wq_bf,
                       mk_p, mv_p, wo_bf, min(s, 512))
    y_p = _moe(x2_p.reshape(b * s, d), g_ffn[0], wr, br, w1_bf, w3_bf, w2_bf, g_final,
               min(b * s, 1024)).reshape(b, s, d)

    pos_s = jnp.full((bd,), past_len, dtype=jnp.int32)
    u_s, qt_s, kt_s, vt_s = _inproj(x_sample.reshape(1, bd, d), pos_s, g_mix[0], wtok_bf, wfeat_bf,
                                    bd, False)
    u_s = u_s.reshape(bd, width)
    q_s, k_s, v_s = qt_s[0].T, kt_s[0].T, vt_s[0].T
    ckt = jnp.transpose(cache_k[0], (0, 2, 3, 1)).reshape(cache_k.shape[1], width, PAGE_SIZE)
    cvt = jnp.transpose(cache_v[0], (0, 2, 3, 1)).reshape(cache_v.shape[1], width, PAGE_SIZE)
    top = _page_gate(ckt, page_table, q_s, k_s)
    oa_s = _moba_sample(ckt, cvt, page_table, top, q_s, k_s, v_s)
    state_t = jnp.transpose(state_pool[0], (1, 0, 2))
    x1_s = _mix_sample(x_sample.reshape(bd, d), u_s, state_t, oa_s, wp_bf, s_pool[0], wout_bf)
    x2_s = _mem_sample(x1_s, g_mem[0], wq_bf, cache_mem_k[0], cache_mem_v[0], wo_bf)
    y_s = _moe(x2_s, g_ffn[0], wr, br, w1_bf, w3_bf, w2_bf, g_final, bd).reshape(bd, 1, d)

    def heads(t_feat, n):
        lead = t_feat.shape[0]
        return jnp.transpose(t_feat.reshape(lead, n_heads, HEAD_DIM, n), (0, 3, 1, 2))[None]

    mem_heads = lambda t: t.reshape(1, b, t.shape[1], -1, MEM_HEAD_DIM)
    new_pool_p = u_p[:, s - POOL_BUF:][None]
    new_pool_s = jnp.concatenate([state_pool[0][:, 1:], u_s[:, None, :]], axis=1)[None]
    new_k_s = jnp.transpose(heads(kt_s, bd), (0, 2, 1, 3, 4))
    new_v_s = jnp.transpose(heads(vt_s, bd), (0, 2, 1, 3, 4))
    return (y_p, y_s, heads(kt_p, s), heads(vt_p, s), new_pool_p, mem_heads(mk_p),
            mem_heads(mv_p), new_k_s, new_v_s, new_pool_s)
```

```python
import functools

import jax
import jax.numpy as jnp
from jax import lax
from jax.experimental import pallas as pl
from jax.experimental.pallas import tpu as pltpu

F32 = jnp.float32
BF16 = jnp.bfloat16

POOL_WINDOWS = (2, 4, 8, 16)
POOL_BUF = max(POOL_WINDOWS) - 1
HEAD_DIM = 64
ROPE_DIM = HEAD_DIM // 4
ROPE_THETA = 500000.0
MOBA_BLOCK = 256
MOBA_TOPK = 3
MOBA_GROUP = 4
PAGE_SIZE = 128
MEM_HEAD_DIM = 128
N_GROUPS = 4
EXPERTS_PER_GROUP = 8
MOE_EXPERTS_PER_STEP = 4
MOE_WINDOW = 128
MOE_CHUNK = 256
RMS_EPS = 1e-6
NEG_INF = -1e30
LOG2_E = 1.4426950408889634

LANES = 128
VMEM_LIMIT = 56 * 1024 * 1024


def _params(*sem):
    return pltpu.CompilerParams(dimension_semantics=sem, vmem_limit_bytes=VMEM_LIMIT)


def _rms(x, g):
    ms = jnp.mean(x * x, axis=-1, keepdims=True)
    return x * lax.rsqrt(ms + RMS_EPS) * g


def _dot(a, b):
    return jnp.dot(a, b, preferred_element_type=F32)


def _dot_nt(a, b):
    return lax.dot_general(a, b, (((1,), (1,)), ((), ())), preferred_element_type=F32)


def _dot_exact(a, b):
    return jnp.dot(a, b, preferred_element_type=F32, precision=lax.Precision.HIGHEST)


def _idiv(x, n):
    shift = n.bit_length() - 1
    assert 1 << shift == n
    return jnp.right_shift(x, shift)


def _const_spec(shape):
    return pl.BlockSpec(shape, lambda *_: (0,) * len(shape))


def _diag_mask(n):
    return (lax.broadcasted_iota(jnp.int32, (n, n), 0) == lax.broadcasted_iota(jnp.int32, (n, n), 1))


def _inproj_kernel(x_ref, g_ref, wtok_ref, wfeat_ref, cos_ref, sa_ref, sb_ref, cos_t_ref,
                   sa_t_ref, sb_t_ref, u_ref, qt_ref, kt_ref, vt_ref, *attn_refs):
    width = u_ref.shape[-1]
    half = ROPE_DIM // 2
    n_pairs = width // LANES
    h = _rms(x_ref[...], g_ref[...]).astype(BF16)

    def rope_feat(t):
        outs = []
        for c in range(n_pairs):
            tc = t[c * LANES:(c + 1) * LANES, :]
            outs.append(tc * cos_t_ref[...] + pltpu.roll(tc, LANES - half, axis=0) * sa_t_ref[...]
                        + pltpu.roll(tc, half, axis=0) * sb_t_ref[...])
        return jnp.concatenate(outs, axis=0)

    def rope_tok(t):
        outs = []
        for c in range(n_pairs):
            tc = t[:, c * LANES:(c + 1) * LANES]
            outs.append(tc * cos_ref[...] + pltpu.roll(tc, LANES - half, axis=1) * sa_ref[...]
                        + pltpu.roll(tc, half, axis=1) * sb_ref[...])
        return jnp.concatenate(outs, axis=1)

    feat = _dot_nt(wfeat_ref[...], h)
    qt_ref[...] = rope_feat(feat[:width]) * (HEAD_DIM ** -0.5)
    kt_ref[...] = rope_feat(feat[width:2 * width])
    vt = feat[2 * width:]
    vt_ref[...] = vt
    if attn_refs:
        kb_ref, vtb_ref = attn_refs
        tok = _dot(h, wtok_ref[...])
        u_ref[...] = tok[:, :width]
        kb_ref[...] = rope_tok(tok[:, width:]).astype(BF16)
        for jj in range(vtb_ref.shape[0]):
            vtb_ref[jj] = vt[:, jj * MOBA_BLOCK:(jj + 1) * MOBA_BLOCK].astype(BF16)
    else:
        u_ref[...] = _dot(h, wtok_ref[:, :width])


def _rope_tables(pos):
    half = ROPE_DIM // 2
    inv_freq = ROPE_THETA ** (-jnp.arange(half, dtype=F32) / half)
    ang = pos.astype(F32)[:, None] * inv_freq[None, :]
    cos, sin = jnp.cos(ang), jnp.sin(ang)
    n = pos.shape[0]
    pad = jnp.zeros((n, HEAD_DIM - ROPE_DIM), F32)
    cos_h = jnp.concatenate([cos, cos, pad + 1.0], axis=1)
    sa_h = jnp.concatenate([-sin, jnp.zeros_like(sin), pad], axis=1)
    sb_h = jnp.concatenate([jnp.zeros_like(sin), sin, pad], axis=1)
    rep = LANES // HEAD_DIM
    return tuple(jnp.tile(t, (1, rep)) for t in (cos_h, sa_h, sb_h))


def _inproj(x, pos, g_mix, wtok_bf, wfeat_bf, tm, attn_copies):
    b, s, d = x.shape
    width = wtok_bf.shape[1] // 2
    tabs = _rope_tables(pos)
    tabs_t = tuple(t.T for t in tabs)
    tok = lambda last: pl.BlockSpec((None, tm, last), lambda bi, i: (bi, i, 0))
    feat = pl.BlockSpec((None, width, tm), lambda bi, i: (bi, 0, i))
    tab = pl.BlockSpec((tm, LANES), lambda bi, i: (i, 0))
    tab_t = pl.BlockSpec((LANES, tm), lambda bi, i: (0, i))
    feat_shape = jax.ShapeDtypeStruct((b, width, s), F32)
    out_specs = [tok(width), feat, feat, feat]
    out_shape = [jax.ShapeDtypeStruct((b, s, width), F32), feat_shape, feat_shape, feat_shape]
    if attn_copies:
        per = tm // MOBA_BLOCK
        out_specs += [tok(width),
                      pl.BlockSpec((None, per, width, MOBA_BLOCK), lambda bi, i: (bi, i, 0, 0))]
        out_shape += [jax.ShapeDtypeStruct((b, s, width), BF16),
                      jax.ShapeDtypeStruct((b, s // MOBA_BLOCK, width, MOBA_BLOCK), BF16)]
    return pl.pallas_call(
        _inproj_kernel,
        grid=(b, s // tm),
        in_specs=[tok(d), _const_spec((1, d)), _const_spec(wtok_bf.shape),
                  _const_spec(wfeat_bf.shape), tab, tab, tab, tab_t, tab_t, tab_t],
        out_specs=out_specs,
        out_shape=out_shape,
        compiler_params=_params("parallel", "parallel"),
        name="inproj",
    )(x, g_mix.reshape(1, d), wtok_bf, wfeat_bf, *tabs, *tabs_t)


def _moba_prompt_kernel(qt_ref, k_ref, vt_ref, vo_ref, o_ref, kmean_sc):
    i = pl.program_id(2)
    nb = vt_ref.shape[0]
    blk = qt_ref.shape[1]

    @pl.when(i == 0)
    def _():
        r = lax.broadcasted_iota(jnp.int32, (nb, nb * blk), 1)
        j = lax.broadcasted_iota(jnp.int32, (nb, nb * blk), 0)
        onehot = jnp.where(_idiv(r, blk) == j, 1.0, 0.0).astype(BF16)
        kmean_sc[...] = _dot(onehot, k_ref[...]) / blk

    qt = qt_ref[...] * LOG2_E
    feat = lax.broadcasted_iota(jnp.int32, (LANES, 1), 0)
    qbd = jnp.concatenate([jnp.where(_idiv(feat, HEAD_DIM) == hd, qt, 0.0) for hd in range(2)],
                          axis=1)
    qb = qbd.astype(BF16)
    blk_id = lax.broadcasted_iota(jnp.int32, (nb, 2 * blk), 0)
    gate = jnp.where(blk_id < i, _dot_exact(kmean_sc[...], qbd), -jnp.inf)
    sel = jnp.zeros((nb, 2 * blk), F32)
    for _ in range(MOBA_TOPK):
        gmax = jnp.max(gate, axis=0, keepdims=True)
        first = jnp.min(jnp.where(gate == gmax, blk_id, nb), axis=0, keepdims=True)
        pick = blk_id == first
        sel = jnp.where(pick, 1.0, sel)
        gate = jnp.where(pick, -jnp.inf, gate)
    sel = jnp.where(blk_id < i, sel, 0.0)

    def attend(st, vts, carry):
        m, l, acc = carry
        m_new = jnp.maximum(m, jnp.max(st, axis=0, keepdims=True))
        alpha = jnp.exp2(m - m_new)
        p = jnp.exp2(st - m_new)
        l_new = alpha * l + jnp.sum(p, axis=0, keepdims=True)
        pb = p.astype(BF16)
        acc = alpha * acc
        for g, vt in enumerate(vts):
            acc = acc + _dot(vt, pb[g * blk:(g + 1) * blk])
        return m_new, l_new, acc

    carry = (jnp.full((1, 2 * blk), NEG_INF, F32), jnp.zeros((1, 2 * blk), F32),
             jnp.zeros((LANES, 2 * blk), F32))
    key_id = lax.broadcasted_iota(jnp.int32, (blk, 2 * blk), 0)
    qry_id = lax.broadcasted_iota(jnp.int32, (blk, 2 * blk), 1) & (blk - 1)
    own0 = pl.multiple_of(i * blk, blk)
    s_own = jnp.where(key_id <= qry_id, _dot(k_ref[pl.ds(own0, blk), :], qb), NEG_INF)
    carry = attend(s_own, [vo_ref[...].astype(BF16)], carry)

    def body(t, carry):
        r0 = pl.multiple_of(t * (MOBA_GROUP * blk), MOBA_GROUP * blk)
        s = _dot(k_ref[pl.ds(r0, MOBA_GROUP * blk), :], qb)
        segs, vts = [], []
        for g in range(MOBA_GROUP):
            jb = t * MOBA_GROUP + g
            chosen = jnp.sum(jnp.where(blk_id == jb, sel, 0.0), axis=0, keepdims=True) > 0.5
            segs.append(jnp.where(chosen, s[g * blk:(g + 1) * blk], NEG_INF))
            vts.append(vt_ref[jb])
        return attend(jnp.concatenate(segs, axis=0), vts, carry)

    n_groups = lax.shift_right_logical(i + (MOBA_GROUP - 1), MOBA_GROUP.bit_length() - 1)
    m, l, acc = lax.fori_loop(0, n_groups, body, carry)
    ot = jnp.concatenate([acc[hd * HEAD_DIM:(hd + 1) * HEAD_DIM, hd * blk:(hd + 1) * blk]
                          / l[:, hd * blk:(hd + 1) * blk] for hd in range(2)], axis=0)
    o_ref[...] = ot.T.astype(o_ref.dtype)


def _moba_prompt(qt, kb, vtb, vt):
    b, width, s = qt.shape
    nb = s // MOBA_BLOCK
    assert nb % MOBA_GROUP == 0
    blk = MOBA_BLOCK
    qspec = pl.BlockSpec((None, LANES, blk), lambda bi, hp, i: (bi, hp, i))
    kspec = pl.BlockSpec((None, s, LANES), lambda bi, hp, i: (bi, 0, hp))
    vspec = pl.BlockSpec((None, nb, LANES, blk), lambda bi, hp, i: (bi, 0, hp, 0))
    ospec = pl.BlockSpec((None, blk, LANES), lambda bi, hp, i: (bi, i, hp))
    return pl.pallas_call(
        _moba_prompt_kernel,
        grid=(b, width // LANES, nb),
        in_specs=[qspec, kspec, vspec, qspec],
        out_specs=ospec,
        out_shape=jax.ShapeDtypeStruct((b, s, width), BF16),
        scratch_shapes=[pltpu.VMEM((nb, LANES), F32)],
        compiler_params=_params("parallel", "parallel", "arbitrary"),
        name="moba_prompt",
    )(qt, kb, vtb, vt)


def _memkv_kernel(mem_ref, g_ref, w_ref, mk_ref, mv_ref):
    width = mk_ref.shape[-1]
    hm = _rms(mem_ref[...], g_ref[...]).astype(BF16)
    kv = _dot(hm, w_ref[...])
    mk_ref[...] = kv[:, :width]
    mv_ref[...] = kv[:, width:]


def _memkv(mem, g_mem_in, w_mkv_bf):
    b, m, d = mem.shape
    width = w_mkv_bf.shape[1] // 2
    out = jax.ShapeDtypeStruct((b, m, width), F32)
    return pl.pallas_call(
        _memkv_kernel,
        grid=(b,),
        in_specs=[pl.BlockSpec((None, m, d), lambda bi: (bi, 0, 0)), _const_spec((1, d)),
                  _const_spec(w_mkv_bf.shape)],
        out_specs=[pl.BlockSpec((None, m, width), lambda bi: (bi, 0, 0))] * 2,
        out_shape=[out, out],
        compiler_params=_params("parallel"),
        name="memkv",
    )(mem, g_mem_in.reshape(1, d), w_mkv_bf)


def _pool_mix(z_in_groups, wp_ref, sp_ref):
    zs = [_dot(d.astype(BF16), wp_ref[g]) for g, d in enumerate(z_in_groups)]
    return jnp.concatenate(zs, axis=1) * sp_ref[...]


def _cross_attend(x1, gm_ref, wq_ref, mk, mv, wo_ref):
    hm = _rms(x1, gm_ref[...]).astype(BF16)
    qm = _dot(hm, wq_ref[...])
    outs = []
    for hh in range(qm.shape[1] // MEM_HEAD_DIM):
        sl = slice(hh * MEM_HEAD_DIM, (hh + 1) * MEM_HEAD_DIM)
        s = _dot_nt(qm[:, sl].astype(BF16), mk[:, sl]) * (MEM_HEAD_DIM ** -0.5)
        p = jnp.exp(s - jnp.max(s, axis=1, keepdims=True))
        p = p / jnp.sum(p, axis=1, keepdims=True)
        outs.append(_dot(p.astype(BF16), mv[:, sl]))
    o = jnp.concatenate(outs, axis=1).astype(BF16)
    return x1 + _dot(o, wo_ref[...])


def _mix_prompt_kernel(x_ref, u_ref, halo_ref, oa_ref, wp_ref, sp_ref, wout_ref, gm_ref, wq_ref,
                       mk_ref, mv_ref, wo_ref, x2_ref):
    i = pl.program_id(1)
    tm, width = u_ref.shape
    gw = width // len(POOL_WINDOWS)
    halo_rows = halo_ref.shape[0]
    u = u_ref[...]
    halo = jnp.where(i > 0, halo_ref[...], 0.0)
    ext = jnp.concatenate([halo, u], axis=0)
    pos = i * tm + lax.broadcasted_iota(jnp.int32, (tm, 1), 0)
    groups = []
    for g, w in enumerate(POOL_WINDOWS):
        a = ext[:, g * gw:(g + 1) * gw]
        span = 1
        while span < w:
            a = a + pltpu.roll(a, span, axis=0)
            span *= 2
        cnt = jnp.minimum(w, pos + 1).astype(F32)
        groups.append(a[halo_rows:] / cnt - u[:, g * gw:(g + 1) * gw])
    z = _pool_mix(groups, wp_ref, sp_ref).astype(BF16)
    x1 = x_ref[...] + _dot(z, wout_ref[:width, :]) + _dot(oa_ref[...], wout_ref[width:, :])
    x2_ref[...] = _cross_attend(x1, gm_ref, wq_ref, mk_ref[...].astype(BF16),
                                mv_ref[...].astype(BF16), wo_ref)


def _mix_prompt(x, u, o_attn, wp_bf, s_pool, wout_bf, g_mem, wq_bf, mk, mv, wo_bf, tm):
    b, s, d = x.shape
    width = u.shape[-1]
    m = mk.shape[1]
    halo = 16
    tok = lambda last: pl.BlockSpec((None, tm, last), lambda bi, i: (bi, i, 0))
    halo_spec = pl.BlockSpec((None, halo, width),
                             lambda bi, i: (bi, jnp.maximum(i * (tm // halo) - 1, 0), 0))
    mem_spec = pl.BlockSpec((None, m, mk.shape[-1]), lambda bi, i: (bi, 0, 0))
    return pl.pallas_call(
        _mix_prompt_kernel,
        grid=(b, s // tm),
        in_specs=[tok(d), tok(width), halo_spec, tok(width), _const_spec(wp_bf.shape),
                  _const_spec((1, width)), _const_spec(wout_bf.shape), _const_spec((1, d)),
                  _const_spec(wq_bf.shape), mem_spec, mem_spec, _const_spec(wo_bf.shape)],
        out_specs=tok(d),
        out_shape=jax.ShapeDtypeStruct((b, s, d), F32),
        compiler_params=_params("parallel", "parallel"),
        name="mix_prompt",
    )(x, u, u, o_attn, wp_bf, s_pool.reshape(1, width), wout_bf, g_mem.reshape(1, d), wq_bf,
      mk, mv, wo_bf)


def _route(x_ref, gf_ref, wr_ref, br_ref):
    t = _rms(x_ref[...], gf_ref[...])
    lane = lax.broadcasted_iota(jnp.int32, (t.shape[0], LANES), 1)
    logits = _dot_exact(t, wr_ref[...]) + br_ref[...]
    n_e = N_GROUPS * EXPERTS_PER_GROUP
    is_g = (lane >= n_e) & (lane < n_e + N_GROUPS)
    lg = jnp.where(is_g, logits, -jnp.inf)
    gmax = jnp.max(lg, axis=1, keepdims=True)
    grp = jnp.min(jnp.where(lg == gmax, lane, LANES), axis=1, keepdims=True) - n_e
    p_grp = 1.0 / jnp.sum(jnp.where(is_g, jnp.exp(logits - gmax), 0.0), axis=1, keepdims=True)
    le = jnp.where(_idiv(lane, EXPERTS_PER_GROUP) == grp, logits, -jnp.inf)
    t1 = jnp.max(le, axis=1, keepdims=True)
    i1 = jnp.min(jnp.where(le == t1, lane, LANES), axis=1, keepdims=True)
    le2 = jnp.where(lane == i1, -jnp.inf, le)
    t2 = jnp.max(le2, axis=1, keepdims=True)
    i2 = jnp.min(jnp.where(le2 == t2, lane, LANES), axis=1, keepdims=True)
    r = jnp.exp(t2 - t1)
    g1 = 1.0 / (1.0 + r)
    g2 = r / (1.0 + r)
    comb = p_grp * (jnp.where(lane == i1, g1, 0.0) + jnp.where(lane == i2, g2, 0.0))
    return t, grp, comb


def _expert_run(h, comb, first_expert, w1_ref, w3_ref, w2_ref):
    run, hid, d = w2_ref.shape
    lane = lax.broadcasted_iota(jnp.int32, comb.shape, 1)
    scaled = []
    for k in range(run):
        a1 = _dot(h, w1_ref[k])
        a3 = _dot(h, w3_ref[k])
        c = jnp.sum(jnp.where(lane == first_expert + k, comb, 0.0), axis=1, keepdims=True)
        scaled.append(((a1 * (1.0 / (1.0 + jnp.exp(-a1)))) * a3 * c).astype(BF16))
    return _dot(jnp.concatenate(scaled, axis=1), w2_ref[...].reshape(run * hid, d))


def _moe_kernel(x_ref, gf_ref, wr_ref, br_ref, w1_ref, w3_ref, w2_ref, gfin_ref, y_ref,
                h_sc, comb_sc, acc_sc):
    e = pl.program_id(1)
    n_runs = pl.num_programs(1)

    @pl.when(e == 0)
    def _():
        t, _, comb = _route(x_ref, gf_ref, wr_ref, br_ref)
        h_sc[...] = t.astype(BF16)
        comb_sc[...] = comb
        acc_sc[...] = jnp.zeros_like(acc_sc)

    acc_sc[...] += _expert_run(h_sc[...], comb_sc[...], e * w2_ref.shape[0], w1_ref, w3_ref, w2_ref)

    @pl.when(e == n_runs - 1)
    def _():
        y_ref[...] = _rms(x_ref[...] + acc_sc[...], gfin_ref[...])


def _moe_sorted_kernel(x_ref, gf_ref, wr_ref, br_ref, w1_ref, w3_ref, w2_ref, gfin_ref, y_ref,
                       hs_sc, cs_sc, ys_sc, dpos_sc, win_sc):
    r = pl.program_id(1)
    n_runs = pl.num_programs(1)
    tm = x_ref.shape[0]
    tms = hs_sc.shape[0]
    run = w2_ref.shape[0]
    win, chunk = MOE_WINDOW, MOE_CHUNK

    @pl.when(r == 0)
    def _():
        t, grp, comb = _route(x_ref, gf_ref, wr_ref, br_ref)
        h = t.astype(BF16)
        lane = lax.broadcasted_iota(jnp.int32, (tm, LANES), 1)
        ind = jnp.where(lane == grp, 1.0, 0.0)
        cnt = jnp.sum(ind, axis=0, keepdims=True)
        n_win = jnp.floor((cnt + (win - 1)) * (1.0 / win))
        lane8 = lax.broadcasted_iota(jnp.int32, (8, LANES), 1)
        padded8 = jnp.broadcast_to(n_win * win, (8, LANES))
        off8 = jnp.zeros((8, LANES), F32)
        for sft in range(1, N_GROUPS):
            off8 = off8 + jnp.where(lane8 >= sft, pltpu.roll(padded8, sft, axis=1), 0.0)
        off = off8[:1]
        ind_b = ind.astype(BF16)
        dpos_cols = []
        for c in range(tm // chunk):
            rows = lax.broadcasted_iota(jnp.int32, (chunk, tm), 0) + c * chunk
            cols = lax.broadcasted_iota(jnp.int32, (chunk, tm), 1)
            earlier = jnp.where(cols < rows, 1.0, 0.0).astype(BF16)
            rank = _dot(earlier, ind_b)
            dpos_cols.append(jnp.sum(ind[c * chunk:(c + 1) * chunk] * (rank + off), axis=1,
                                     keepdims=True))
        dpos_sc[...] = jnp.broadcast_to(jnp.concatenate(dpos_cols, axis=0), (tm, LANES))
        diag = _diag_mask(chunk)
        dpos_row = jnp.concatenate(
            [jnp.sum(jnp.where(diag, dc, 0.0), axis=0, keepdims=True) for dc in dpos_cols], axis=1)
        comb_hi = comb.astype(BF16)
        comb_lo = (comb - comb_hi.astype(F32)).astype(BF16)
        for c in range(tms // chunk):
            dst = (lax.broadcasted_iota(jnp.int32, (chunk, 1), 0) + c * chunk).astype(F32)
            gather = jnp.where(dpos_row == dst, 1.0, 0.0).astype(BF16)
            hs_sc[c * chunk:(c + 1) * chunk, :] = _dot(gather, h).astype(BF16)
            cs_sc[c * chunk:(c + 1) * chunk, :] = _dot(gather, comb_hi) + _dot(gather, comb_lo)
        ys_sc[...] = jnp.zeros_like(ys_sc)
        first_win = (off * (1.0 / win)).astype(jnp.int32)
        n_win_i = n_win.astype(jnp.int32)
        for g in range(N_GROUPS):
            win_sc[g] = first_win[0, g]
            win_sc[N_GROUPS + g] = n_win_i[0, g]

    first_expert = r * run
    g = lax.shift_right_logical(first_expert, EXPERTS_PER_GROUP.bit_length() - 1)
    w0 = win_sc[g]

    def body(w, carry):
        r0 = pl.multiple_of((w0 + w) * win, win)
        rows = pl.ds(r0, win)
        ys_sc[rows, :] += _expert_run(hs_sc[rows, :], cs_sc[rows, :], first_expert,
                                      w1_ref, w3_ref, w2_ref)
        return carry

    lax.fori_loop(0, win_sc[N_GROUPS + g], body, 0)

    @pl.when(r == n_runs - 1)
    def _():
        ys_b = ys_sc[...].astype(BF16)
        col = lax.broadcasted_iota(jnp.int32, (1, tms), 1).astype(F32)
        for c in range(tm // chunk):
            rows = slice(c * chunk, (c + 1) * chunk)
            scatter = jnp.where(col == dpos_sc[rows, :1], 1.0, 0.0).astype(BF16)
            y_ref[rows, :] = _rms(x_ref[rows, :] + _dot(scatter, ys_b), gfin_ref[...])


def _moe(x, g_ffn, wr, br, w1_bf, w3_bf, w2_bf, g_final, tm):
    t, d = x.shape
    n_exp, _, hid = w1_bf.shape
    run = MOE_EXPERTS_PER_STEP
    assert n_exp % run == 0
    assert EXPERTS_PER_GROUP % run == 0
    tok = pl.BlockSpec((tm, d), lambda i, e: (i, 0))
    sort_by_group = tm % MOE_CHUNK == 0 and tm >= 2 * N_GROUPS * MOE_WINDOW
    if sort_by_group:
        tms = tm + N_GROUPS * MOE_WINDOW
        assert tms % MOE_CHUNK == 0
        kern = _moe_sorted_kernel
        scratch = [pltpu.VMEM((tms, d), BF16), pltpu.VMEM((tms, LANES), F32),
                   pltpu.VMEM((tms, d), F32), pltpu.VMEM((tm, LANES), F32),
                   pltpu.SMEM((2 * N_GROUPS,), jnp.int32)]
    else:
        kern = _moe_kernel
        scratch = [pltpu.VMEM((tm, d), BF16), pltpu.VMEM((tm, LANES), F32),
                   pltpu.VMEM((tm, d), F32)]
    return pl.pallas_call(
        kern,
        grid=(t // tm, n_exp // run),
        in_specs=[tok, _const_spec((1, d)), _const_spec(wr.shape), _const_spec((1, LANES)),
                  pl.BlockSpec((run, d, hid), lambda i, e: (e, 0, 0)),
                  pl.BlockSpec((run, d, hid), lambda i, e: (e, 0, 0)),
                  pl.BlockSpec((run, hid, d), lambda i, e: (e, 0, 0)),
                  _const_spec((1, d))],
        out_specs=tok,
        out_shape=jax.ShapeDtypeStruct((t, d), F32),
        scratch_shapes=scratch,
        compiler_params=_params("parallel", "arbitrary"),
        name="moe",
    )(x, g_ffn.reshape(1, d), wr, br, w1_bf, w3_bf, w2_bf, g_final.reshape(1, d))


def _page_gate_kernel(pt_ref, *refs, pages_per_step, n_blocks):
    page_refs = refs[:pages_per_step]
    q_ref, kn_ref, top_ref, ksum_sc = refs[pages_per_step:]
    c = pl.program_id(1)
    width = q_ref.shape[-1]
    n_heads = width // HEAD_DIM
    per_blk = MOBA_BLOCK // PAGE_SIZE
    n_cols = pages_per_step // per_blk
    lane = lax.broadcasted_iota(jnp.int32, (1, LANES), 1)

    @pl.when(c == 0)
    def _():
        ksum_sc[...] = jnp.zeros_like(ksum_sc)

    upd = ksum_sc[...]
    for r in range(n_cols):
        pages = [page_refs[r * per_blk + g][...] for g in range(per_blk)]
        col = jnp.sum(functools.reduce(lambda a, b: a + b, pages), axis=1, keepdims=True)
        upd = jnp.where(lane == c * n_cols + r, col, upd)
    ksum_sc[...] = upd

    @pl.when(c == pl.num_programs(1) - 1)
    def _():
        hrow = lax.broadcasted_iota(jnp.int32, (n_heads, width), 0)
        feat = lax.broadcasted_iota(jnp.int32, (n_heads, width), 1)
        qblk = jnp.where(_idiv(feat, HEAD_DIM) == hrow, q_ref[...], 0.0)
        blk = lax.broadcasted_iota(jnp.int32, (n_heads, LANES), 1)
        gate_past = _dot_exact(qblk, ksum_sc[...] / MOBA_BLOCK)
        gate_own = jnp.sum(qblk * (kn_ref[...] / MOBA_BLOCK), axis=1, keepdims=True)
        gate = jnp.where(blk == n_blocks, gate_own, gate_past)
        gate = jnp.where(blk < n_blocks, gate, -jnp.inf)
        slot = lax.broadcasted_iota(jnp.int32, (n_heads, MOBA_TOPK), 1)
        top = jnp.zeros((n_heads, MOBA_TOPK), jnp.int32)
        for t in range(MOBA_TOPK):
            gmax = jnp.max(gate, axis=1, keepdims=True)
            first = jnp.min(jnp.where(gate == gmax, blk, LANES), axis=1, keepdims=True)
            top = jnp.where(slot == t, first, top)
            gate = jnp.where(blk == first, -jnp.inf, gate)
        top_ref[...] = top


def _page_gate(cache_kt, page_table, q_s, k_new):
    bd, n_pages = page_table.shape
    width = q_s.shape[-1]
    n_blocks = n_pages * PAGE_SIZE // MOBA_BLOCK
    assert n_blocks < LANES
    pages_per_step = min(32, n_pages)
    assert n_pages % pages_per_step == 0
    steps = n_pages // pages_per_step
    n_heads = width // HEAD_DIM

    def page_spec(j):
        return pl.BlockSpec(
            (None, width, PAGE_SIZE),
            lambda b, c, pt: (pt[b * n_pages + c * pages_per_step + j], 0, 0))

    row = pl.BlockSpec((None, 1, width), lambda b, c, pt: (b, 0, 0))
    grid_spec = pltpu.PrefetchScalarGridSpec(
        num_scalar_prefetch=1,
        grid=(bd, steps),
        in_specs=[page_spec(j) for j in range(pages_per_step)] + [row, row],
        out_specs=pl.BlockSpec((None, n_heads, MOBA_TOPK), lambda b, c, pt: (b, 0, 0)),
        scratch_shapes=[pltpu.VMEM((width, LANES), F32)],
    )
    kern = functools.partial(_page_gate_kernel, pages_per_step=pages_per_step, n_blocks=n_blocks)
    return pl.pallas_call(
        kern,
        grid_spec=grid_spec,
        out_shape=jax.ShapeDtypeStruct((bd, n_heads, MOBA_TOPK), jnp.int32),
        compiler_params=_params("parallel", "arbitrary"),
        name="page_gate",
    )(page_table.reshape(-1), *([cache_kt] * pages_per_step), q_s.reshape(bd, 1, width),
      k_new.reshape(bd, 1, width))


def _moba_sample_kernel(pt_ref, top_ref, ck_hbm, cv_hbm, q_ref, kn_ref, vn_ref, o_ref,
                        kbuf, vbuf, sems, *, n_pages):
    b = pl.program_id(0)
    n_samples = pl.num_programs(0)
    width = q_ref.shape[-1]
    n_heads = width // HEAD_DIM
    per_blk = MOBA_BLOCK // PAGE_SIZE
    n_sel_pages = MOBA_TOPK * per_blk

    def tile_copies(page, head, slot, idx):
        rows = pl.ds(head * HEAD_DIM, HEAD_DIM)
        return (pltpu.make_async_copy(ck_hbm.at[page, rows, :], kbuf.at[slot, idx], sems.at[slot, 0]),
                pltpu.make_async_copy(cv_hbm.at[page, rows, :], vbuf.at[slot, idx], sems.at[slot, 1]))

    def start_sample(sample, slot):
        for head in range(n_heads):
            for t in range(MOBA_TOPK):
                blk = top_ref[(sample * n_heads + head) * MOBA_TOPK + t]
                for pg in range(per_blk):
                    page = pt_ref[sample * n_pages + blk * per_blk + pg]
                    for cp in tile_copies(page, head, slot, (head * MOBA_TOPK + t) * per_blk + pg):
                        cp.start()

    def wait_sample(slot):
        for idx in range(n_heads * n_sel_pages):
            for cp in tile_copies(0, 0, slot, idx):
                cp.wait()

    slot = lax.rem(b, 2)

    @pl.when(b == 0)
    def _():
        start_sample(0, 0)

    @pl.when(b + 1 < n_samples)
    def _():
        start_sample(b + 1, 1 - slot)

    wait_sample(slot)

    lane = lax.broadcasted_iota(jnp.int32, (1, LANES), 1)
    diag = _diag_mask(LANES)
    first = lane < HEAD_DIM
    outs = []
    for hp in range(width // LANES):
        sl = slice(hp * LANES, (hp + 1) * LANES)
        q = q_ref[:, sl]
        kn = kn_ref[:, sl]
        qcol = jnp.sum(jnp.where(diag, q, 0.0), axis=1, keepdims=True)
        ocols, p_owns, dens = [], [], []
        for hd in range(2):
            base = (2 * hp + hd) * n_sel_pages
            qc = qcol[hd * HEAD_DIM:(hd + 1) * HEAD_DIM]
            in_head = _idiv(lane, HEAD_DIM) == hd
            s_own = jnp.sum(jnp.where(in_head, q * kn, 0.0), axis=1, keepdims=True)
            scores = [jnp.sum(kbuf[slot, base + r] * qc, axis=0, keepdims=True)
                      for r in range(n_sel_pages)]
            m = s_own
            for s in scores:
                m = jnp.maximum(m, jnp.max(s, axis=1, keepdims=True))
            p_own = jnp.exp(s_own - m)
            den = p_own
            acc = jnp.zeros((HEAD_DIM, PAGE_SIZE), F32)
            for r, s in enumerate(scores):
                p = jnp.exp(s - m)
                den = den + jnp.sum(p, axis=1, keepdims=True)
                acc = acc + vbuf[slot, base + r] * p
            ocols.append(jnp.sum(acc, axis=1, keepdims=True))
            p_owns.append(p_own)
            dens.append(den)
        ocol = jnp.concatenate(ocols, axis=0)
        orow = jnp.sum(jnp.where(diag, ocol, 0.0), axis=0, keepdims=True)
        p_own = jnp.where(first, p_owns[0], p_owns[1])
        den = jnp.where(first, dens[0], dens[1])
        outs.append((orow + p_own * vn_ref[:, sl]) / den)
    o_ref[...] = jnp.concatenate(outs, axis=1)


def _moba_sample(cache_kt, cache_vt, page_table, top, q_s, k_new, v_new):
    bd, n_pages = page_table.shape
    width = q_s.shape[-1]
    n_heads = width // HEAD_DIM
    n_tiles = n_heads * MOBA_TOPK * (MOBA_BLOCK // PAGE_SIZE)
    row = pl.BlockSpec((None, 1, width), lambda b, pt, tp: (b, 0, 0))
    hbm = pl.BlockSpec(memory_space=pl.ANY)
    grid_spec = pltpu.PrefetchScalarGridSpec(
        num_scalar_prefetch=2,
        grid=(bd,),
        in_specs=[hbm, hbm, row, row, row],
        out_specs=row,
        scratch_shapes=[pltpu.VMEM((2, n_tiles, HEAD_DIM, PAGE_SIZE), F32),
                        pltpu.VMEM((2, n_tiles, HEAD_DIM, PAGE_SIZE), F32),
                        pltpu.SemaphoreType.DMA((2, 2))],
    )
    kern = functools.partial(_moba_sample_kernel, n_pages=n_pages)
    as_rows = lambda t: t.reshape(bd, 1, width)
    out = pl.pallas_call(
        kern,
        grid_spec=grid_spec,
        out_shape=jax.ShapeDtypeStruct((bd, 1, width), F32),
        compiler_params=_params("arbitrary"),
        name="moba_sample",
    )(page_table.reshape(-1), top.reshape(-1), cache_kt, cache_vt,
      as_rows(q_s), as_rows(k_new), as_rows(v_new))
    return out.reshape(bd, width)


def _mix_sample_kernel(x_ref, u_ref, st_ref, oa_ref, wp_ref, sp_ref, wout_ref, x1_ref):
    width = u_ref.shape[-1]
    gw = width // len(POOL_WINDOWS)
    u = u_ref[...]
    groups = []
    for g, w in enumerate(POOL_WINDOWS):
        sl = slice(g * gw, (g + 1) * gw)
        a = u[:, sl]
        for back in range(1, w):
            a = a + st_ref[POOL_BUF - back][:, sl]
        groups.append(a / float(w) - u[:, sl])
    z = _pool_mix(groups, wp_ref, sp_ref).astype(BF16)
    x1_ref[...] = (x_ref[...] + _dot(z, wout_ref[:width, :])
                   + _dot(oa_ref[...].astype(BF16), wout_ref[width:, :]))


def _mix_sample(x, u, state_t, o_attn, wp_bf, s_pool, wout_bf):
    t, d = x.shape
    width = u.shape[-1]
    return pl.pallas_call(
        _mix_sample_kernel,
        out_shape=jax.ShapeDtypeStruct((t, d), F32),
        compiler_params=pltpu.CompilerParams(vmem_limit_bytes=VMEM_LIMIT),
        name="mix_sample",
    )(x, u, state_t, o_attn, wp_bf, s_pool.reshape(1, width), wout_bf)


def _mem_sample_kernel(x1_ref, gm_ref, wq_ref, mk_ref, mv_ref, wo_ref, x2_ref):
    x1 = x1_ref[...]
    n = x1.shape[0]
    n_mem_heads = mk_ref.shape[2]
    hm = _rms(x1, gm_ref[...]).astype(BF16)
    qm = _dot(hm, wq_ref[...])
    rows = []
    for r in range(n):
        outs = []
        for hh in range(n_mem_heads):
            sl = slice(hh * MEM_HEAD_DIM, (hh + 1) * MEM_HEAD_DIM)
            kh = mk_ref[r, :, hh, :]
            vh = mv_ref[r, :, hh, :]
            s = jnp.sum(kh * qm[r:r + 1, sl], axis=1, keepdims=True) * (MEM_HEAD_DIM ** -0.5)
            p = jnp.exp(s - jnp.max(s, axis=0, keepdims=True))
            p = p / jnp.sum(p, axis=0, keepdims=True)
            outs.append(jnp.sum(p * vh, axis=0, keepdims=True))
        rows.append(jnp.concatenate(outs, axis=1))
    o = jnp.concatenate(rows, axis=0).astype(BF16)
    x2_ref[...] = x1 + _dot(o, wo_ref[...])


def _mem_sample(x1, g_mem, wq_bf, mem_k, mem_v, wo_bf):
    t, d = x1.shape
    m, mh, md = mem_k.shape[1:]
    n = 8
    tok = pl.BlockSpec((n, d), lambda i: (i, 0))
    mem = pl.BlockSpec((n, m, mh, md), lambda i: (i, 0, 0, 0))
    return pl.pallas_call(
        _mem_sample_kernel,
        grid=(t // n,),
        in_specs=[tok, _const_spec((1, d)), _const_spec(wq_bf.shape), mem, mem,
                  _const_spec(wo_bf.shape)],
        out_specs=tok,
        out_shape=jax.ShapeDtypeStruct((t, d), F32),
        compiler_params=_params("parallel"),
        name="mem_sample",
    )(x1, g_mem.reshape(1, d), wq_bf, mem_k, mem_v, wo_bf)


def kernel(x_prompt, x_sample, mem_prompt, cache_k, cache_v, cache_mem_k, cache_mem_v, state_pool,
           page_table, g_mix, w_in, w_pool, s_pool, w_out, g_mem, g_mem_in, w_mq, w_mk, w_mv, w_mo,
           g_ffn, w_rg, b_rg, w_re, b_re, w1, w3, w2, g_final):
    assert g_mix.shape[0] == 1, "one layer"
    b, s, d = x_prompt.shape
    bd, n_new, _ = x_sample.shape
    assert n_new == 1 and s % MOBA_BLOCK == 0
    n_pages = page_table.shape[1]
    past_len = n_pages * PAGE_SIZE
    assert past_len % MOBA_BLOCK == 0 and past_len // MOBA_BLOCK >= MOBA_TOPK
    width = w_in.shape[2] // 4
    n_exp = w1.shape[1]
    n_heads = width // HEAD_DIM

    w_in0 = w_in[0]
    wtok_bf = jnp.concatenate([w_in0[:, :width], w_in0[:, 2 * width:3 * width]], axis=1).astype(BF16)
    wfeat_bf = w_in0[:, width:].T.astype(BF16)
    wp_bf = w_pool[0].astype(BF16)
    wout_bf = w_out[0].astype(BF16)
    wq_bf = w_mq[0].astype(BF16)
    wmkv_bf = jnp.concatenate([w_mk[0], w_mv[0]], axis=1).astype(BF16)
    wo_bf = w_mo[0].astype(BF16)
    w1_bf, w3_bf, w2_bf = w1[0].astype(BF16), w3[0].astype(BF16), w2[0].astype(BF16)
    n_grp = w_rg.shape[2]
    wr = jnp.zeros((d, LANES), F32).at[:, :n_exp].set(w_re[0]).at[:, n_exp:n_exp + n_grp].set(w_rg[0])
    br = jnp.zeros((1, LANES), F32).at[0, :n_exp].set(b_re[0]).at[0, n_exp:n_exp + n_grp].set(b_rg[0])

    mk_p, mv_p = _memkv(mem_prompt, g_mem_in[0], wmkv_bf)
    pos_p = jnp.arange(s, dtype=jnp.int32)
    u_p, qt_p, kt_p, vt_p, kb_p, vtb_p = _inproj(x_prompt, pos_p, g_mix[0], wtok_bf, wfeat_bf,
                                                 min(s, 512), True)
    oa_p = _moba_prompt(qt_p, kb_p, vtb_p, vt_p)
    x2_p = _mix_prompt(x_prompt, u_p, oa_p, wp_bf, s_pool[0], wout_bf, g_mem[0], wq_bf,
                       mk_p, mv_p, wo_bf, min(s, 512))
    y_p = _moe(x2_p.reshape(b * s, d), g_ffn[0], wr, br, w1_bf, w3_bf, w2_bf, g_final,
               min(b * s, 1024)).reshape(b, s, d)

    pos_s = jnp.full((bd,), past_len, dtype=jnp.int32)
    u_s, qt_s, kt_s, vt_s = _inproj(x_sample.reshape(1, bd, d), pos_s, g_mix[0], wtok_bf, wfeat_bf,
                                    bd, False)
    u_s = u_s.reshape(bd, width)
    q_s, k_s, v_s = qt_s[0].T, kt_s[0].T, vt_s[0].T
    ckt = jnp.transpose(cache_k[0], (0, 2, 3, 1)).reshape(cache_k.shape[1], width, PAGE_SIZE)
    cvt = jnp.transpose(cache_v[0], (0, 2, 3, 1)).reshape(cache_v.shape[1], width, PAGE_SIZE)
    top = _page_gate(ckt, page_table, q_s, k_s)
    oa_s = _moba_sample(ckt, cvt, page_table, top, q_s, k_s, v_s)
    state_t = jnp.transpose(state_pool[0], (1, 0, 2))
    x1_s = _mix_sample(x_sample.reshape(bd, d), u_s, state_t, oa_s, wp_bf, s_pool[0], wout_bf)
    x2_s = _mem_sample(x1_s, g_mem[0], wq_bf, cache_mem_k[0], cache_mem_v[0], wo_bf)
    y_s = _moe(x2_s, g_ffn[0], wr, br, w1_bf, w3_bf, w2_bf, g_final, bd).reshape(bd, 1, d)

    def heads(t_feat, n):
        lead = t_feat.shape[0]
        return jnp.transpose(t_feat.reshape(lead, n_heads, HEAD_DIM, n), (0, 3, 1, 2))[None]

    mem_heads = lambda t: t.reshape(1, b, t.shape[1], -1, MEM_HEAD_DIM)
    new_pool_p = u_p[:, s - POOL_BUF:][None]
    new_pool_s = jnp.concatenate([state_pool[0][:, 1:], u_s[:, None, :]], axis=1)[None]
    new_k_s = jnp.transpose(heads(kt_s, bd), (0, 2, 1, 3, 4))
    new_v_s = jnp.transpose(heads(vt_s, bd), (0, 2, 1, 3, 4))
    return (y_p, y_s, heads(kt_p, s), heads(vt_p, s), new_pool_p, mem_heads(mk_p),
            mem_heads(mv_p), new_k_s, new_v_s, new_pool_s)
```

```python
import functools

import jax
import jax.numpy as jnp
from jax import lax
from jax.experimental import pallas as pl
from jax.experimental.pallas import tpu as pltpu

F32 = jnp.float32
BF16 = jnp.bfloat16

POOL_WINDOWS = (2, 4, 8, 16)
POOL_BUF = max(POOL_WINDOWS) - 1
HEAD_DIM = 64
ROPE_DIM = HEAD_DIM // 4
ROPE_THETA = 500000.0
MOBA_BLOCK = 256
MOBA_TOPK = 3
MOBA_GROUP = 4
PAGE_SIZE = 128
MEM_HEAD_DIM = 128
N_GROUPS = 4
EXPERTS_PER_GROUP = 8
MOE_EXPERTS_PER_STEP = 4
MOE_WINDOW = 128
MOE_CHUNK = 256
RMS_EPS = 1e-6
NEG_INF = -1e30
LOG2_E = 1.4426950408889634

LANES = 128
VMEM_LIMIT = 56 * 1024 * 1024


def _params(*sem):
    return pltpu.CompilerParams(dimension_semantics=sem, vmem_limit_bytes=VMEM_LIMIT)


def _rms(x, g):
    ms = jnp.mean(x * x, axis=-1, keepdims=True)
    return x * lax.rsqrt(ms + RMS_EPS) * g


def _dot(a, b):
    return jnp.dot(a, b, preferred_element_type=F32)


def _dot_nt(a, b):
    return lax.dot_general(a, b, (((1,), (1,)), ((), ())), preferred_element_type=F32)


def _dot_exact(a, b):
    return jnp.dot(a, b, preferred_element_type=F32, precision=lax.Precision.HIGHEST)


def _idiv(x, n):
    shift = n.bit_length() - 1
    assert 1 << shift == n
    return jnp.right_shift(x, shift)


def _const_spec(shape):
    return pl.BlockSpec(shape, lambda *_: (0,) * len(shape))


def _diag_mask(n):
    return (lax.broadcasted_iota(jnp.int32, (n, n), 0) == lax.broadcasted_iota(jnp.int32, (n, n), 1))


def _inproj_kernel(x_ref, g_ref, wtok_ref, wfeat_ref, cos_ref, sa_ref, sb_ref, cos_t_ref,
                   sa_t_ref, sb_t_ref, u_ref, qt_ref, kt_ref, vt_ref, *attn_refs):
    width = u_ref.shape[-1]
    half = ROPE_DIM // 2
    n_pairs = width // LANES
    h = _rms(x_ref[...], g_ref[...]).astype(BF16)

    def rope_feat(t):
        outs = []
        for c in range(n_pairs):
            tc = t[c * LANES:(c + 1) * LANES, :]
            outs.append(tc * cos_t_ref[...] + pltpu.roll(tc, LANES - half, axis=0) * sa_t_ref[...]
                        + pltpu.roll(tc, half, axis=0) * sb_t_ref[...])
        return jnp.concatenate(outs, axis=0)

    def rope_tok(t):
        outs = []
        for c in range(n_pairs):
            tc = t[:, c * LANES:(c + 1) * LANES]
            outs.append(tc * cos_ref[...] + pltpu.roll(tc, LANES - half, axis=1) * sa_ref[...]
                        + pltpu.roll(tc, half, axis=1) * sb_ref[...])
        return jnp.concatenate(outs, axis=1)

    feat = _dot_nt(wfeat_ref[...], h)
    qt_ref[...] = rope_feat(feat[:width]) * (HEAD_DIM ** -0.5)
    kt_ref[...] = rope_feat(feat[width:2 * width])
    vt = feat[2 * width:]
    vt_ref[...] = vt
    if attn_refs:
        kb_ref, vtb_ref = attn_refs
        tok = _dot(h, wtok_ref[...])
        u_ref[...] = tok[:, :width]
        kb_ref[...] = rope_tok(tok[:, width:]).astype(BF16)
        for jj in range(vtb_ref.shape[0]):
            vtb_ref[jj] = vt[:, jj * MOBA_BLOCK:(jj + 1) * MOBA_BLOCK].astype(BF16)
    else:
        u_ref[...] = _dot(h, wtok_ref[:, :width])


def _rope_tables(pos):
    half = ROPE_DIM // 2
    inv_freq = ROPE_THETA ** (-jnp.arange(half, dtype=F32) / half)
    ang = pos.astype(F32)[:, None] * inv_freq[None, :]
    cos, sin = jnp.cos(ang), jnp.sin(ang)
    n = pos.shape[0]
    pad = jnp.zeros((n, HEAD_DIM - ROPE_DIM), F32)
    cos_h = jnp.concatenate([cos, cos, pad + 1.0], axis=1)
    sa_h = jnp.concatenate([-sin, jnp.zeros_like(sin), pad], axis=1)
    sb_h = jnp.concatenate([jnp.zeros_like(sin), sin, pad], axis=1)
    rep = LANES // HEAD_DIM
    return tuple(jnp.tile(t, (1, rep)) for t in (cos_h, sa_h, sb_h))


def _inproj(x, pos, g_mix, wtok_bf, wfeat_bf, tm, attn_copies):
    b, s, d = x.shape
    width = wtok_bf.shape[1] // 2
    tabs = _rope_tables(pos)
    tabs_t = tuple(t.T for t in tabs)
    tok = lambda last: pl.BlockSpec((None, tm, last), lambda bi, i: (bi, i, 0))
    feat = pl.BlockSpec((None, width, tm), lambda bi, i: (bi, 0, i))
    tab = pl.BlockSpec((tm, LANES), lambda bi, i: (i, 0))
    tab_t = pl.BlockSpec((LANES, tm), lambda bi, i: (0, i))
    feat_shape = jax.ShapeDtypeStruct((b, width, s), F32)
    out_specs = [tok(width), feat, feat, feat]
    out_shape = [jax.ShapeDtypeStruct((b, s, width), F32), feat_shape, feat_shape, feat_shape]
    if attn_copies:
        per = tm // MOBA_BLOCK
        out_specs += [tok(width),
                      pl.BlockSpec((None, per, width, MOBA_BLOCK), lambda bi, i: (bi, i, 0, 0))]
        out_shape += [jax.ShapeDtypeStruct((b, s, width), BF16),
                      jax.ShapeDtypeStruct((b, s // MOBA_BLOCK, width, MOBA_BLOCK), BF16)]
    return pl.pallas_call(
        _inproj_kernel,
        grid=(b, s // tm),
        in_specs=[tok(d), _const_spec((1, d)), _const_spec(wtok_bf.shape),
                  _const_spec(wfeat_bf.shape), tab, tab, tab, tab_t, tab_t, tab_t],
        out_specs=out_specs,
        out_shape=out_shape,
        compiler_params=_params("parallel", "parallel"),
        name="inproj",
    )(x, g_mix.reshape(1, d), wtok_bf, wfeat_bf, *tabs, *tabs_t)


def _moba_prompt_kernel(qt_ref, k_ref, vt_ref, vo_ref, o_ref, kmean_sc):
    i = pl.program_id(2)
    nb = vt_ref.shape[0]
    blk = qt_ref.shape[1]

    @pl.when(i == 0)
    def _():
        r = lax.broadcasted_iota(jnp.int32, (nb, nb * blk), 1)
        j = lax.broadcasted_iota(jnp.int32, (nb, nb * blk), 0)
        onehot = jnp.where(_idiv(r, blk) == j, 1.0, 0.0).astype(BF16)
        kmean_sc[...] = _dot(onehot, k_ref[...]) / blk

    qt = qt_ref[...] * LOG2_E
    feat = lax.broadcasted_iota(jnp.int32, (LANES, 1), 0)
    qbd = jnp.concatenate([jnp.where(_idiv(feat, HEAD_DIM) == hd, qt, 0.0) for hd in range(2)],
                          axis=1)
    qb = qbd.astype(BF16)
    blk_id = lax.broadcasted_iota(jnp.int32, (nb, 2 * blk), 0)
    gate = jnp.where(blk_id < i, _dot_exact(kmean_sc[...], qbd), -jnp.inf)
    sel = jnp.zeros((nb, 2 * blk), F32)
    for _ in range(MOBA_TOPK):
        gmax = jnp.max(gate, axis=0, keepdims=True)
        first = jnp.min(jnp.where(gate == gmax, blk_id, nb), axis=0, keepdims=True)
        pick = blk_id == first
        sel = jnp.where(pick, 1.0, sel)
        gate = jnp.where(pick, -jnp.inf, gate)
    sel = jnp.where(blk_id < i, sel, 0.0)

    def attend(st, vts, carry):
        m, l, acc = carry
        m_new = jnp.maximum(m, jnp.max(st, axis=0, keepdims=True))
        alpha = jnp.exp2(m - m_new)
        p = jnp.exp2(st - m_new)
        l_new = alpha * l + jnp.sum(p, axis=0, keepdims=True)
        pb = p.astype(BF16)
        acc = alpha * acc
        for g, vt in enumerate(vts):
            acc = acc + _dot(vt, pb[g * blk:(g + 1) * blk])
        return m_new, l_new, acc

    carry = (jnp.full((1, 2 * blk), NEG_INF, F32), jnp.zeros((1, 2 * blk), F32),
             jnp.zeros((LANES, 2 * blk), F32))
    key_id = lax.broadcasted_iota(jnp.int32, (blk, 2 * blk), 0)
    qry_id = lax.broadcasted_iota(jnp.int32, (blk, 2 * blk), 1) & (blk - 1)
    own0 = pl.multiple_of(i * blk, blk)
    s_own = jnp.where(key_id <= qry_id, _dot(k_ref[pl.ds(own0, blk), :], qb), NEG_INF)
    carry = attend(s_own, [vo_ref[...].astype(BF16)], carry)

    def group(first_blk, n_blk, carry):
        r0 = pl.multiple_of(first_blk * blk, blk)
        s = _dot(k_ref[pl.ds(r0, n_blk * blk), :], qb)
        segs, vts = [], []
        for g in range(n_blk):
            jb = first_blk + g
            chosen = jnp.sum(jnp.where(blk_id == jb, sel, 0.0), axis=0, keepdims=True) > 0.5
            segs.append(jnp.where(chosen, s[g * blk:(g + 1) * blk], NEG_INF))
            vts.append(vt_ref[jb])
        return attend(jnp.concatenate(segs, axis=0), vts, carry)

    half = MOBA_GROUP // 2
    n_full = lax.shift_right_logical(i + 1, MOBA_GROUP.bit_length() - 1)
    carry = lax.fori_loop(0, n_full, lambda t, c: group(t * MOBA_GROUP, MOBA_GROUP, c), carry)
    n_half = jnp.where(i > n_full * MOBA_GROUP, 1, 0)
    m, l, acc = lax.fori_loop(0, n_half, lambda t, c: group(n_full * MOBA_GROUP, half, c), carry)
    ot = jnp.concatenate([acc[hd * HEAD_DIM:(hd + 1) * HEAD_DIM, hd * blk:(hd + 1) * blk]
                          / l[:, hd * blk:(hd + 1) * blk] for hd in range(2)], axis=0)
    o_ref[...] = ot.T.astype(o_ref.dtype)


def _moba_prompt(qt, kb, vtb, vt):
    b, width, s = qt.shape
    nb = s // MOBA_BLOCK
    assert nb % MOBA_GROUP == 0
    blk = MOBA_BLOCK
    qspec = pl.BlockSpec((None, LANES, blk), lambda bi, hp, i: (bi, hp, i))
    kspec = pl.BlockSpec((None, s, LANES), lambda bi, hp, i: (bi, 0, hp))
    vspec = pl.BlockSpec((None, nb, LANES, blk), lambda bi, hp, i: (bi, 0, hp, 0))
    ospec = pl.BlockSpec((None, blk, LANES), lambda bi, hp, i: (bi, i, hp))
    return pl.pallas_call(
        _moba_prompt_kernel,
        grid=(b, width // LANES, nb),
        in_specs=[qspec, kspec, vspec, qspec],
        out_specs=ospec,
        out_shape=jax.ShapeDtypeStruct((b, s, width), BF16),
        scratch_shapes=[pltpu.VMEM((nb, LANES), F32)],
        compiler_params=_params("parallel", "parallel", "arbitrary"),
        name="moba_prompt",
    )(qt, kb, vtb, vt)


def _memkv_kernel(mem_ref, g_ref, w_ref, mk_ref, mv_ref):
    width = mk_ref.shape[-1]
    hm = _rms(mem_ref[...], g_ref[...]).astype(BF16)
    kv = _dot(hm, w_ref[...])
    mk_ref[...] = kv[:, :width]
    mv_ref[...] = kv[:, width:]


def _memkv(mem, g_mem_in, w_mkv_bf):
    b, m, d = mem.shape
    width = w_mkv_bf.shape[1] // 2
    out = jax.ShapeDtypeStruct((b, m, width), F32)
    return pl.pallas_call(
        _memkv_kernel,
        grid=(b,),
        in_specs=[pl.BlockSpec((None, m, d), lambda bi: (bi, 0, 0)), _const_spec((1, d)),
                  _const_spec(w_mkv_bf.shape)],
        out_specs=[pl.BlockSpec((None, m, width), lambda bi: (bi, 0, 0))] * 2,
        out_shape=[out, out],
        compiler_params=_params("parallel"),
        name="memkv",
    )(mem, g_mem_in.reshape(1, d), w_mkv_bf)


def _pool_mix(z_in_groups, wp_ref, sp_ref):
    zs = [_dot(d.astype(BF16), wp_ref[g]) for g, d in enumerate(z_in_groups)]
    return jnp.concatenate(zs, axis=1) * sp_ref[...]


def _cross_attend(x1, gm_ref, wq_ref, mk, mv, wo_ref):
    hm = _rms(x1, gm_ref[...]).astype(BF16)
    qm = _dot(hm, wq_ref[...])
    outs = []
    for hh in range(qm.shape[1] // MEM_HEAD_DIM):
        sl = slice(hh * MEM_HEAD_DIM, (hh + 1) * MEM_HEAD_DIM)
        s = _dot_nt(qm[:, sl].astype(BF16), mk[:, sl]) * (MEM_HEAD_DIM ** -0.5)
        p = jnp.exp(s - jnp.max(s, axis=1, keepdims=True))
        p = p / jnp.sum(p, axis=1, keepdims=True)
        outs.append(_dot(p.astype(BF16), mv[:, sl]))
    o = jnp.concatenate(outs, axis=1).astype(BF16)
    return x1 + _dot(o, wo_ref[...])


def _mix_prompt_kernel(x_ref, u_ref, halo_ref, oa_ref, wp_ref, sp_ref, wout_ref, gm_ref, wq_ref,
                       mk_ref, mv_ref, wo_ref, x2_ref):
    i = pl.program_id(1)
    tm, width = u_ref.shape
    gw = width // len(POOL_WINDOWS)
    halo_rows = halo_ref.shape[0]
    u = u_ref[...]
    halo = jnp.where(i > 0, halo_ref[...], 0.0)
    ext = jnp.concatenate([halo, u], axis=0)
    pos = i * tm + lax.broadcasted_iota(jnp.int32, (tm, 1), 0)
    groups = []
    for g, w in enumerate(POOL_WINDOWS):
        a = ext[:, g * gw:(g + 1) * gw]
        span = 1
        while span < w:
            a = a + pltpu.roll(a, span, axis=0)
            span *= 2
        cnt = jnp.minimum(w, pos + 1).astype(F32)
        groups.append(a[halo_rows:] / cnt - u[:, g * gw:(g + 1) * gw])
    z = _pool_mix(groups, wp_ref, sp_ref).astype(BF16)
    x1 = x_ref[...] + _dot(z, wout_ref[:width, :]) + _dot(oa_ref[...], wout_ref[width:, :])
    x2_ref[...] = _cross_attend(x1, gm_ref, wq_ref, mk_ref[...].astype(BF16),
                                mv_ref[...].astype(BF16), wo_ref)


def _mix_prompt(x, u, o_attn, wp_bf, s_pool, wout_bf, g_mem, wq_bf, mk, mv, wo_bf, tm):
    b, s, d = x.shape
    width = u.shape[-1]
    m = mk.shape[1]
    halo = 16
    tok = lambda last: pl.BlockSpec((None, tm, last), lambda bi, i: (bi, i, 0))
    halo_spec = pl.BlockSpec((None, halo, width),
                             lambda bi, i: (bi, jnp.maximum(i * (tm // halo) - 1, 0), 0))
    mem_spec = pl.BlockSpec((None, m, mk.shape[-1]), lambda bi, i: (bi, 0, 0))
    return pl.pallas_call(
        _mix_prompt_kernel,
        grid=(b, s // tm),
        in_specs=[tok(d), tok(width), halo_spec, tok(width), _const_spec(wp_bf.shape),
                  _const_spec((1, width)), _const_spec(wout_bf.shape), _const_spec((1, d)),
                  _const_spec(wq_bf.shape), mem_spec, mem_spec, _const_spec(wo_bf.shape)],
        out_specs=tok(d),
        out_shape=jax.ShapeDtypeStruct((b, s, d), F32),
        compiler_params=_params("parallel", "parallel"),
        name="mix_prompt",
    )(x, u, u, o_attn, wp_bf, s_pool.reshape(1, width), wout_bf, g_mem.reshape(1, d), wq_bf,
      mk, mv, wo_bf)


def _route(x_ref, gf_ref, wr_ref, br_ref):
    t = _rms(x_ref[...], gf_ref[...])
    lane = lax.broadcasted_iota(jnp.int32, (t.shape[0], LANES), 1)
    logits = _dot_exact(t, wr_ref[...]) + br_ref[...]
    n_e = N_GROUPS * EXPERTS_PER_GROUP
    is_g = (lane >= n_e) & (lane < n_e + N_GROUPS)
    lg = jnp.where(is_g, logits, -jnp.inf)
    gmax = jnp.max(lg, axis=1, keepdims=True)
    grp = jnp.min(jnp.where(lg == gmax, lane, LANES), axis=1, keepdims=True) - n_e
    p_grp = 1.0 / jnp.sum(jnp.where(is_g, jnp.exp(logits - gmax), 0.0), axis=1, keepdims=True)
    le = jnp.where(_idiv(lane, EXPERTS_PER_GROUP) == grp, logits, -jnp.inf)
    t1 = jnp.max(le, axis=1, keepdims=True)
    i1 = jnp.min(jnp.where(le == t1, lane, LANES), axis=1, keepdims=True)
    le2 = jnp.where(lane == i1, -jnp.inf, le)
    t2 = jnp.max(le2, axis=1, keepdims=True)
    i2 = jnp.min(jnp.where(le2 == t2, lane, LANES), axis=1, keepdims=True)
    r = jnp.exp(t2 - t1)
    g1 = 1.0 / (1.0 + r)
    g2 = r / (1.0 + r)
    comb = p_grp * (jnp.where(lane == i1, g1, 0.0) + jnp.where(lane == i2, g2, 0.0))
    return t, grp, comb


def _expert_run(h, comb, first_expert, w1_ref, w3_ref, w2_ref):
    run, hid, d = w2_ref.shape
    lane = lax.broadcasted_iota(jnp.int32, comb.shape, 1)
    scaled = []
    for k in range(run):
        a1 = _dot(h, w1_ref[k])
        a3 = _dot(h, w3_ref[k])
        c = jnp.sum(jnp.where(lane == first_expert + k, comb, 0.0), axis=1, keepdims=True)
        scaled.append(((a1 * (1.0 / (1.0 + jnp.exp(-a1)))) * a3 * c).astype(BF16))
    return _dot(jnp.concatenate(scaled, axis=1), w2_ref[...].reshape(run * hid, d))


def _moe_kernel(x_ref, gf_ref, wr_ref, br_ref, w1_ref, w3_ref, w2_ref, gfin_ref, y_ref,
                h_sc, comb_sc, acc_sc):
    e = pl.program_id(1)
    n_runs = pl.num_programs(1)

    @pl.when(e == 0)
    def _():
        t, _, comb = _route(x_ref, gf_ref, wr_ref, br_ref)
        h_sc[...] = t.astype(BF16)
        comb_sc[...] = comb
        acc_sc[...] = jnp.zeros_like(acc_sc)

    acc_sc[...] += _expert_run(h_sc[...], comb_sc[...], e * w2_ref.shape[0], w1_ref, w3_ref, w2_ref)

    @pl.when(e == n_runs - 1)
    def _():
        y_ref[...] = _rms(x_ref[...] + acc_sc[...], gfin_ref[...])


def _moe_sorted_kernel(x_ref, gf_ref, wr_ref, br_ref, w1_ref, w3_ref, w2_ref, gfin_ref, y_ref,
                       hs_sc, cs_sc, ys_sc, dpos_sc, win_sc):
    r = pl.program_id(1)
    n_runs = pl.num_programs(1)
    tm = x_ref.shape[0]
    tms = hs_sc.shape[0]
    run = w2_ref.shape[0]
    win, chunk = MOE_WINDOW, MOE_CHUNK

    @pl.when(r == 0)
    def _():
        t, grp, comb = _route(x_ref, gf_ref, wr_ref, br_ref)
        h = t.astype(BF16)
        lane = lax.broadcasted_iota(jnp.int32, (tm, LANES), 1)
        ind = jnp.where(lane == grp, 1.0, 0.0)
        cnt = jnp.sum(ind, axis=0, keepdims=True)
        n_win = jnp.floor((cnt + (win - 1)) * (1.0 / win))
        lane8 = lax.broadcasted_iota(jnp.int32, (8, LANES), 1)
        padded8 = jnp.broadcast_to(n_win * win, (8, LANES))
        off8 = jnp.zeros((8, LANES), F32)
        for sft in range(1, N_GROUPS):
            off8 = off8 + jnp.where(lane8 >= sft, pltpu.roll(padded8, sft, axis=1), 0.0)
        off = off8[:1]
        ind_b = ind.astype(BF16)
        dpos_cols = []
        for c in range(tm // chunk):
            rows = lax.broadcasted_iota(jnp.int32, (chunk, tm), 0) + c * chunk
            cols = lax.broadcasted_iota(jnp.int32, (chunk, tm), 1)
            earlier = jnp.where(cols < rows, 1.0, 0.0).astype(BF16)
            rank = _dot(earlier, ind_b)
            dpos_cols.append(jnp.sum(ind[c * chunk:(c + 1) * chunk] * (rank + off), axis=1,
                                     keepdims=True))
        dpos_sc[...] = jnp.broadcast_to(jnp.concatenate(dpos_cols, axis=0), (tm, LANES))
        diag = _diag_mask(chunk)
        dpos_row = jnp.concatenate(
            [jnp.sum(jnp.where(diag, dc, 0.0), axis=0, keepdims=True) for dc in dpos_cols], axis=1)
        comb_hi = comb.astype(BF16)
        comb_lo = (comb - comb_hi.astype(F32)).astype(BF16)
        for c in range(tms // chunk):
            dst = (lax.broadcasted_iota(jnp.int32, (chunk, 1), 0) + c * chunk).astype(F32)
            gather = jnp.where(dpos_row == dst, 1.0, 0.0).astype(BF16)
            hs_sc[c * chunk:(c + 1) * chunk, :] = _dot(gather, h).astype(BF16)
            cs_sc[c * chunk:(c + 1) * chunk, :] = _dot(gather, comb_hi) + _dot(gather, comb_lo)
        ys_sc[...] = jnp.zeros_like(ys_sc)
        first_win = (off * (1.0 / win)).astype(jnp.int32)
        n_win_i = n_win.astype(jnp.int32)
        for g in range(N_GROUPS):
            win_sc[g] = first_win[0, g]
            win_sc[N_GROUPS + g] = n_win_i[0, g]

    first_expert = r * run
    g = lax.shift_right_logical(first_expert, EXPERTS_PER_GROUP.bit_length() - 1)
    w0 = win_sc[g]

    def body(w, carry):
        r0 = pl.multiple_of((w0 + w) * win, win)
        rows = pl.ds(r0, win)
        ys_sc[rows, :] += _expert_run(hs_sc[rows, :], cs_sc[rows, :], first_expert,
                                      w1_ref, w3_ref, w2_ref)
        return carry

    lax.fori_loop(0, win_sc[N_GROUPS + g], body, 0)

    @pl.when(r == n_runs - 1)
    def _():
        ys_b = ys_sc[...].astype(BF16)
        col = lax.broadcasted_iota(jnp.int32, (1, tms), 1).astype(F32)
        for c in range(tm // chunk):
            rows = slice(c * chunk, (c + 1) * chunk)
            scatter = jnp.where(col == dpos_sc[rows, :1], 1.0, 0.0).astype(BF16)
            y_ref[rows, :] = _rms(x_ref[rows, :] + _dot(scatter, ys_b), gfin_ref[...])


def _moe(x, g_ffn, wr, br, w1_bf, w3_bf, w2_bf, g_final, tm):
    t, d = x.shape
    n_exp, _, hid = w1_bf.shape
    run = MOE_EXPERTS_PER_STEP
    assert n_exp % run == 0
    assert EXPERTS_PER_GROUP % run == 0
    tok = pl.BlockSpec((tm, d), lambda i, e: (i, 0))
    sort_by_group = tm % MOE_CHUNK == 0 and tm >= 2 * N_GROUPS * MOE_WINDOW
    if sort_by_group:
        tms = tm + N_GROUPS * MOE_WINDOW
        assert tms % MOE_CHUNK == 0
        kern = _moe_sorted_kernel
        scratch = [pltpu.VMEM((tms, d), BF16), pltpu.VMEM((tms, LANES), F32),
                   pltpu.VMEM((tms, d), F32), pltpu.VMEM((tm, LANES), F32),
                   pltpu.SMEM((2 * N_GROUPS,), jnp.int32)]
    else:
        kern = _moe_kernel
        scratch = [pltpu.VMEM((tm, d), BF16), pltpu.VMEM((tm, LANES), F32),
                   pltpu.VMEM((tm, d), F32)]
    return pl.pallas_call(
        kern,
        grid=(t // tm, n_exp // run),
        in_specs=[tok, _const_spec((1, d)), _const_spec(wr.shape), _const_spec((1, LANES)),
                  pl.BlockSpec((run, d, hid), lambda i, e: (e, 0, 0)),
                  pl.BlockSpec((run, d, hid), lambda i, e: (e, 0, 0)),
                  pl.BlockSpec((run, hid, d), lambda i, e: (e, 0, 0)),
                  _const_spec((1, d))],
        out_specs=tok,
        out_shape=jax.ShapeDtypeStruct((t, d), F32),
        scratch_shapes=scratch,
        compiler_params=_params("parallel", "arbitrary"),
        name="moe",
    )(x, g_ffn.reshape(1, d), wr, br, w1_bf, w3_bf, w2_bf, g_final.reshape(1, d))


def _page_gate_kernel(pt_ref, *refs, pages_per_step, n_blocks):
    page_refs = refs[:pages_per_step]
    q_ref, kn_ref, top_ref, ksum_sc = refs[pages_per_step:]
    c = pl.program_id(1)
    width = q_ref.shape[-1]
    n_heads = width // HEAD_DIM
    per_blk = MOBA_BLOCK // PAGE_SIZE
    n_cols = pages_per_step // per_blk
    lane = lax.broadcasted_iota(jnp.int32, (1, LANES), 1)

    @pl.when(c == 0)
    def _():
        ksum_sc[...] = jnp.zeros_like(ksum_sc)

    upd = ksum_sc[...]
    for r in range(n_cols):
        pages = [page_refs[r * per_blk + g][...] for g in range(per_blk)]
        col = jnp.sum(functools.reduce(lambda a, b: a + b, pages), axis=1, keepdims=True)
        upd = jnp.where(lane == c * n_cols + r, col, upd)
    ksum_sc[...] = upd

    @pl.when(c == pl.num_programs(1) - 1)
    def _():
        hrow = lax.broadcasted_iota(jnp.int32, (n_heads, width), 0)
        feat = lax.broadcasted_iota(jnp.int32, (n_heads, width), 1)
        qblk = jnp.where(_idiv(feat, HEAD_DIM) == hrow, q_ref[...], 0.0)
        blk = lax.broadcasted_iota(jnp.int32, (n_heads, LANES), 1)
        gate_past = _dot_exact(qblk, ksum_sc[...] / MOBA_BLOCK)
        gate_own = jnp.sum(qblk * (kn_ref[...] / MOBA_BLOCK), axis=1, keepdims=True)
        gate = jnp.where(blk == n_blocks, gate_own, gate_past)
        gate = jnp.where(blk < n_blocks, gate, -jnp.inf)
        slot = lax.broadcasted_iota(jnp.int32, (n_heads, MOBA_TOPK), 1)
        top = jnp.zeros((n_heads, MOBA_TOPK), jnp.int32)
        for t in range(MOBA_TOPK):
            gmax = jnp.max(gate, axis=1, keepdims=True)
            first = jnp.min(jnp.where(gate == gmax, blk, LANES), axis=1, keepdims=True)
            top = jnp.where(slot == t, first, top)
            gate = jnp.where(blk == first, -jnp.inf, gate)
        top_ref[...] = top


def _page_gate(cache_kt, page_table, q_s, k_new):
    bd, n_pages = page_table.shape
    width = q_s.shape[-1]
    n_blocks = n_pages * PAGE_SIZE // MOBA_BLOCK
    assert n_blocks < LANES
    pages_per_step = min(32, n_pages)
    assert n_pages % pages_per_step == 0
    steps = n_pages // pages_per_step
    n_heads = width // HEAD_DIM

    def page_spec(j):
        return pl.BlockSpec(
            (None, width, PAGE_SIZE),
            lambda b, c, pt: (pt[b * n_pages + c * pages_per_step + j], 0, 0))

    row = pl.BlockSpec((None, 1, width), lambda b, c, pt: (b, 0, 0))
    grid_spec = pltpu.PrefetchScalarGridSpec(
        num_scalar_prefetch=1,
        grid=(bd, steps),
        in_specs=[page_spec(j) for j in range(pages_per_step)] + [row, row],
        out_specs=pl.BlockSpec((None, n_heads, MOBA_TOPK), lambda b, c, pt: (b, 0, 0)),
        scratch_shapes=[pltpu.VMEM((width, LANES), F32)],
    )
    kern = functools.partial(_page_gate_kernel, pages_per_step=pages_per_step, n_blocks=n_blocks)
    return pl.pallas_call(
        kern,
        grid_spec=grid_spec,
        out_shape=jax.ShapeDtypeStruct((bd, n_heads, MOBA_TOPK), jnp.int32),
        compiler_params=_params("parallel", "arbitrary"),
        name="page_gate",
    )(page_table.reshape(-1), *([cache_kt] * pages_per_step), q_s.reshape(bd, 1, width),
      k_new.reshape(bd, 1, width))


def _moba_sample_kernel(pt_ref, top_ref, ck_hbm, cv_hbm, q_ref, kn_ref, vn_ref, o_ref,
                        kbuf, vbuf, sems, *, n_pages):
    b = pl.program_id(0)
    n_samples = pl.num_programs(0)
    width = q_ref.shape[-1]
    n_heads = width // HEAD_DIM
    per_blk = MOBA_BLOCK // PAGE_SIZE
    n_sel_pages = MOBA_TOPK * per_blk

    def tile_copies(page, head, slot, idx):
        rows = pl.ds(head * HEAD_DIM, HEAD_DIM)
        return (pltpu.make_async_copy(ck_hbm.at[page, rows, :], kbuf.at[slot, idx], sems.at[slot, 0]),
                pltpu.make_async_copy(cv_hbm.at[page, rows, :], vbuf.at[slot, idx], sems.at[slot, 1]))

    def start_sample(sample, slot):
        for head in range(n_heads):
            for t in range(MOBA_TOPK):
                blk = top_ref[(sample * n_heads + head) * MOBA_TOPK + t]
                for pg in range(per_blk):
                    page = pt_ref[sample * n_pages + blk * per_blk + pg]
                    for cp in tile_copies(page, head, slot, (head * MOBA_TOPK + t) * per_blk + pg):
                        cp.start()

    def wait_sample(slot):
        for idx in range(n_heads * n_sel_pages):
            for cp in tile_copies(0, 0, slot, idx):
                cp.wait()

    slot = lax.rem(b, 2)

    @pl.when(b == 0)
    def _():
        start_sample(0, 0)

    @pl.when(b + 1 < n_samples)
    def _():
        start_sample(b + 1, 1 - slot)

    wait_sample(slot)

    lane = lax.broadcasted_iota(jnp.int32, (1, LANES), 1)
    diag = _diag_mask(LANES)
    first = lane < HEAD_DIM
    outs = []
    for hp in range(width // LANES):
        sl = slice(hp * LANES, (hp + 1) * LANES)
        q = q_ref[:, sl]
        kn = kn_ref[:, sl]
        qcol = jnp.sum(jnp.where(diag, q, 0.0), axis=1, keepdims=True)
        ocols, p_owns, dens = [], [], []
        for hd in range(2):
            base = (2 * hp + hd) * n_sel_pages
            qc = qcol[hd * HEAD_DIM:(hd + 1) * HEAD_DIM]
            in_head = _idiv(lane, HEAD_DIM) == hd
            s_own = jnp.sum(jnp.where(in_head, q * kn, 0.0), axis=1, keepdims=True)
            scores = [jnp.sum(kbuf[slot, base + r] * qc, axis=0, keepdims=True)
                      for r in range(n_sel_pages)]
            m = s_own
            for s in scores:
                m = jnp.maximum(m, jnp.max(s, axis=1, keepdims=True))
            p_own = jnp.exp(s_own - m)
            den = p_own
            acc = jnp.zeros((HEAD_DIM, PAGE_SIZE), F32)
            for r, s in enumerate(scores):
                p = jnp.exp(s - m)
                den = den + jnp.sum(p, axis=1, keepdims=True)
                acc = acc + vbuf[slot, base + r] * p
            ocols.append(jnp.sum(acc, axis=1, keepdims=True))
            p_owns.append(p_own)
            dens.append(den)
        ocol = jnp.concatenate(ocols, axis=0)
        orow = jnp.sum(jnp.where(diag, ocol, 0.0), axis=0, keepdims=True)
        p_own = jnp.where(first, p_owns[0], p_owns[1])
        den = jnp.where(first, dens[0], dens[1])
        outs.append((orow + p_own * vn_ref[:, sl]) / den)
    o_ref[...] = jnp.concatenate(outs, axis=1)


def _moba_sample(cache_kt, cache_vt, page_table, top, q_s, k_new, v_new):
    bd, n_pages = page_table.shape
    width = q_s.shape[-1]
    n_heads = width // HEAD_DIM
    n_tiles = n_heads * MOBA_TOPK * (MOBA_BLOCK // PAGE_SIZE)
    row = pl.BlockSpec((None, 1, width), lambda b, pt, tp: (b, 0, 0))
    hbm = pl.BlockSpec(memory_space=pl.ANY)
    grid_spec = pltpu.PrefetchScalarGridSpec(
        num_scalar_prefetch=2,
        grid=(bd,),
        in_specs=[hbm, hbm, row, row, row],
        out_specs=row,
        scratch_shapes=[pltpu.VMEM((2, n_tiles, HEAD_DIM, PAGE_SIZE), F32),
                        pltpu.VMEM((2, n_tiles, HEAD_DIM, PAGE_SIZE), F32),
                        pltpu.SemaphoreType.DMA((2, 2))],
    )
    kern = functools.partial(_moba_sample_kernel, n_pages=n_pages)
    as_rows = lambda t: t.reshape(bd, 1, width)
    out = pl.pallas_call(
        kern,
        grid_spec=grid_spec,
        out_shape=jax.ShapeDtypeStruct((bd, 1, width), F32),
        compiler_params=_params("arbitrary"),
        name="moba_sample",
    )(page_table.reshape(-1), top.reshape(-1), cache_kt, cache_vt,
      as_rows(q_s), as_rows(k_new), as_rows(v_new))
    return out.reshape(bd, width)


def _mix_sample_kernel(x_ref, u_ref, st_ref, oa_ref, wp_ref, sp_ref, wout_ref, x1_ref):
    width = u_ref.shape[-1]
    gw = width // len(POOL_WINDOWS)
    u = u_ref[...]
    groups = []
    for g, w in enumerate(POOL_WINDOWS):
        sl = slice(g * gw, (g + 1) * gw)
        a = u[:, sl]
        for back in range(1, w):
            a = a + st_ref[POOL_BUF - back][:, sl]
        groups.append(a / float(w) - u[:, sl])
    z = _pool_mix(groups, wp_ref, sp_ref).astype(BF16)
    x1_ref[...] = (x_ref[...] + _dot(z, wout_ref[:width, :])
                   + _dot(oa_ref[...].astype(BF16), wout_ref[width:, :]))


def _mix_sample(x, u, state_t, o_attn, wp_bf, s_pool, wout_bf):
    t, d = x.shape
    width = u.shape[-1]
    return pl.pallas_call(
        _mix_sample_kernel,
        out_shape=jax.ShapeDtypeStruct((t, d), F32),
        compiler_params=pltpu.CompilerParams(vmem_limit_bytes=VMEM_LIMIT),
        name="mix_sample",
    )(x, u, state_t, o_attn, wp_bf, s_pool.reshape(1, width), wout_bf)


def _mem_sample_kernel(x1_ref, gm_ref, wq_ref, mk_ref, mv_ref, wo_ref, x2_ref):
    x1 = x1_ref[...]
    n = x1.shape[0]
    n_mem_heads = mk_ref.shape[2]
    hm = _rms(x1, gm_ref[...]).astype(BF16)
    qm = _dot(hm, wq_ref[...])
    rows = []
    for r in range(n):
        outs = []
        for hh in range(n_mem_heads):
            sl = slice(hh * MEM_HEAD_DIM, (hh + 1) * MEM_HEAD_DIM)
            kh = mk_ref[r, :, hh, :]
            vh = mv_ref[r, :, hh, :]
            s = jnp.sum(kh * qm[r:r + 1, sl], axis=1, keepdims=True) * (MEM_HEAD_DIM ** -0.5)
            p = jnp.exp(s - jnp.max(s, axis=0, keepdims=True))
            p = p / jnp.sum(p, axis=0, keepdims=True)
            outs.append(jnp.sum(p * vh, axis=0, keepdims=True))
        rows.append(jnp.concatenate(outs, axis=1))
    o = jnp.concatenate(rows, axis=0).astype(BF16)
    x2_ref[...] = x1 + _dot(o, wo_ref[...])


def _mem_sample(x1, g_mem, wq_bf, mem_k, mem_v, wo_bf):
    t, d = x1.shape
    m, mh, md = mem_k.shape[1:]
    n = 8
    tok = pl.BlockSpec((n, d), lambda i: (i, 0))
    mem = pl.BlockSpec((n, m, mh, md), lambda i: (i, 0, 0, 0))
    return pl.pallas_call(
        _mem_sample_kernel,
        grid=(t // n,),
        in_specs=[tok, _const_spec((1, d)), _const_spec(wq_bf.shape), mem, mem,
                  _const_spec(wo_bf.shape)],
        out_specs=tok,
        out_shape=jax.ShapeDtypeStruct((t, d), F32),
        compiler_params=_params("parallel"),
        name="mem_sample",
    )(x1, g_mem.reshape(1, d), wq_bf, mem_k, mem_v, wo_bf)


def kernel(x_prompt, x_sample, mem_prompt, cache_k, cache_v, cache_mem_k, cache_mem_v, state_pool,
           page_table, g_mix, w_in, w_pool, s_pool, w_out, g_mem, g_mem_in, w_mq, w_mk, w_mv, w_mo,
           g_ffn, w_rg, b_rg, w_re, b_re, w1, w3, w2, g_final):
    assert g_mix.shape[0] == 1, "one layer"
    b, s, d = x_prompt.shape
    bd, n_new, _ = x_sample.shape
    assert n_new == 1 and s % MOBA_BLOCK == 0
    n_pages = page_table.shape[1]
    past_len = n_pages * PAGE_SIZE
    assert past_len % MOBA_BLOCK == 0 and past_len // MOBA_BLOCK >= MOBA_TOPK
    width = w_in.shape[2] // 4
    n_exp = w1.shape[1]
    n_heads = width // HEAD_DIM

    w_in0 = w_in[0]
    wtok_bf = jnp.concatenate([w_in0[:, :width], w_in0[:, 2 * width:3 * width]], axis=1).astype(BF16)
    wfeat_bf = w_in0[:, width:].T.astype(BF16)
    wp_bf = w_pool[0].astype(BF16)
    wout_bf = w_out[0].astype(BF16)
    wq_bf = w_mq[0].astype(BF16)
    wmkv_bf = jnp.concatenate([w_mk[0], w_mv[0]], axis=1).astype(BF16)
    wo_bf = w_mo[0].astype(BF16)
    w1_bf, w3_bf, w2_bf = w1[0].astype(BF16), w3[0].astype(BF16), w2[0].astype(BF16)
    n_grp = w_rg.shape[2]
    wr = jnp.zeros((d, LANES), F32).at[:, :n_exp].set(w_re[0]).at[:, n_exp:n_exp + n_grp].set(w_rg[0])
    br = jnp.zeros((1, LANES), F32).at[0, :n_exp].set(b_re[0]).at[0, n_exp:n_exp + n_grp].set(b_rg[0])

    mk_p, mv_p = _memkv(mem_prompt, g_mem_in[0], wmkv_bf)
    pos_p = jnp.arange(s, dtype=jnp.int32)
    u_p, qt_p, kt_p, vt_p, kb_p, vtb_p = _inproj(x_prompt, pos_p, g_mix[0], wtok_bf, wfeat_bf,
                                                 min(s, 512), True)
    oa_p = _moba_prompt(qt_p, kb_p, vtb_p, vt_p)
    x2_p = _mix_prompt(x_prompt, u_p, oa_p, wp_bf, s_pool[0], wout_bf, g_mem[0], wq_bf,
                       mk_p, mv_p, wo_bf, min(s, 512))
    y_p = _moe(x2_p.reshape(b * s, d), g_ffn[0], wr, br, w1_bf, w3_bf, w2_bf, g_final,
               min(b * s, 1024)).reshape(b, s, d)

    pos_s = jnp.full((bd,), past_len, dtype=jnp.int32)
    u_s, qt_s, kt_s, vt_s = _inproj(x_sample.reshape(1, bd, d), pos_s, g_mix[0], wtok_bf, wfeat_bf,
                                    bd, False)
    u_s = u_s.reshape(bd, width)
    q_s, k_s, v_s = qt_s[0].T, kt_s[0].T, vt_s[0].T
    ckt = jnp.transpose(cache_k[0], (0, 2, 3, 1)).reshape(cache_k.shape[1], width, PAGE_SIZE)
    cvt = jnp.transpose(cache_v[0], (0, 2, 3, 1)).reshape(cache_v.shape[1], width, PAGE_SIZE)
    top = _page_gate(ckt, page_table, q_s, k_s)
    oa_s = _moba_sample(ckt, cvt, page_table, top, q_s, k_s, v_s)
    state_t = jnp.transpose(state_pool[0], (1, 0, 2))
    x1_s = _mix_sample(x_sample.reshape(bd, d), u_s, state_t, oa_s, wp_bf, s_pool[0], wout_bf)
    x2_s = _mem_sample(x1_s, g_mem[0], wq_bf, cache_mem_k[0], cache_mem_v[0], wo_bf)
    y_s = _moe(x2_s, g_ffn[0], wr, br, w1_bf, w3_bf, w2_bf, g_final, bd).reshape(bd, 1, d)

    def heads(t_feat, n):
        lead = t_feat.shape[0]
        return jnp.transpose(t_feat.reshape(lead, n_heads, HEAD_DIM, n), (0, 3, 1, 2))[None]

    mem_heads = lambda t: t.reshape(1, b, t.shape[1], -1, MEM_HEAD_DIM)
    new_pool_p = u_p[:, s - POOL_BUF:][None]
    new_pool_s = jnp.concatenate([state_pool[0][:, 1:], u_s[:, None, :]], axis=1)[None]
    new_k_s = jnp.transpose(heads(kt_s, bd), (0, 2, 1, 3, 4))
    new_v_s = jnp.transpose(heads(vt_s, bd), (0, 2, 1, 3, 4))
    return (y_p, y_s, heads(kt_p, s), heads(vt_p, s), new_pool_p, mem_heads(mk_p),
            mem_heads(mv_p), new_k_s, new_v_s, new_pool_s)
```
